```python
import math
import jax, jax.numpy as jnp
from jax import lax
import numpy as np

D_MODEL = 2048
BATCH = 1
SEQ = 16384
DEPTH = 1
DEC_BATCH = 32
DEC_SEQ = 1
PAST_LEN = 16384
PAGE_SIZE = 128

N_HEADS = 8
HEAD_DIM = 64
V_DIM = 2 * HEAD_DIM
QK_COLS = N_HEADS * 2 * HEAD_DIM
ATTN_WIDTH = N_HEADS * V_DIM
CONV_CH = D_MODEL // 2
CONV_WIDTH = 31
CONV_STATE = CONV_WIDTH - 1
D_FF = 5504
ROPE_THETA = 500000.0
ROT_DIM = HEAD_DIM // 4
Q_BLOCK = 128
N_MOD = 3
N_SUBLAYERS = 3
EPS = 1e-6
NEG_INF = -1e30
SPLIT_POINTS = (QK_COLS, 2 * QK_COLS, 2 * QK_COLS + ATTN_WIDTH,
                2 * QK_COLS + ATTN_WIDTH + 2 * CONV_CH,
                2 * QK_COLS + ATTN_WIDTH + 2 * CONV_CH + D_MODEL)
IN_COLS = SPLIT_POINTS[-1] + D_MODEL

kernel_name = "diffattn_conformer_gated_hybrid_step"


def lambda_init_of(layer_idx):
    return 0.8 - 0.6 * math.exp(-0.3 * layer_idx)


def rmsnorm(x, g):
    xf = x.astype(jnp.float32)
    y = xf * lax.rsqrt(jnp.mean(xf * xf, axis=-1, keepdims=True) + EPS)
    return (y * g.astype(jnp.float32)).astype(x.dtype)


def layernorm(x, g, b):
    xf = x.astype(jnp.float32)
    mu = jnp.mean(xf, axis=-1, keepdims=True)
    xc = xf - mu
    var = jnp.mean(xc * xc, axis=-1, keepdims=True)
    return (xc * lax.rsqrt(var + EPS) * g.astype(jnp.float32) + b.astype(jnp.float32)).astype(x.dtype)


def modulate(n, shift, scale):
    return n * (1.0 + scale) + shift


def swiglu(x, w_gate, w_up, w_down):
    return (jax.nn.silu(x @ w_gate) * (x @ w_up)) @ w_down


def rope_partial(x, pos):
    half = ROT_DIM // 2
    inv = jnp.power(ROPE_THETA, -jnp.arange(half, dtype=jnp.float32) * 2.0 / ROT_DIM)
    ang = pos.astype(jnp.float32)[:, None] * inv[None, :]
    cos = jnp.cos(ang)[:, None, None, :]
    sin = jnp.sin(ang)[:, None, None, :]
    xf = x.astype(jnp.float32)
    x1 = xf[..., :half]
    x2 = xf[..., half:ROT_DIM]
    out = jnp.concatenate([x1 * cos - x2 * sin, x2 * cos + x1 * sin, xf[..., ROT_DIM:]], axis=-1)
    return out.astype(x.dtype)


def diff_attend_block(q, k, v, lam, q_pos, k_pos):
    s = jnp.einsum('bqhmd,bkhmd->bhmqk', q, k, preferred_element_type=jnp.float32) * (HEAD_DIM ** -0.5)
    s = jnp.where(q_pos[:, None] >= k_pos[None, :], s, NEG_INF)
    p = jax.nn.softmax(s, axis=-1)
    pd = (p[:, :, 0] - lam * p[:, :, 1]).astype(v.dtype)
    return jnp.einsum('bhqk,bkhe->bqhe', pd, v)


def prompt_diff_attention(q, k, v, lam, pos):
    B, T = q.shape[0], q.shape[1]
    nb = T // Q_BLOCK
    qb = q.reshape(B, nb, Q_BLOCK, N_HEADS, 2, HEAD_DIM).transpose(1, 0, 2, 3, 4, 5)
    pb = pos.reshape(nb, Q_BLOCK)
    out = lax.map(lambda a: diff_attend_block(a[0], k, v, lam, a[1], pos), (qb, pb))
    return out.transpose(1, 0, 2, 3, 4).reshape(B, T, N_HEADS, V_DIM)


def sample_diff_attention(q, k_new, v_new, lam, k_past, v_past):
    past_len = k_past.shape[1]
    tq = q.shape[1]
    scale = HEAD_DIM ** -0.5
    s_past = jnp.einsum('bqhmd,bkhmd->bhmqk', q, k_past, preferred_element_type=jnp.float32) * scale
    s_new = jnp.einsum('bqhmd,bkhmd->bhmqk', q, k_new, preferred_element_type=jnp.float32) * scale
    causal = jnp.arange(tq)[:, None] >= jnp.arange(tq)[None, :]
    s_new = jnp.where(causal, s_new, NEG_INF)
    p = jax.nn.softmax(jnp.concatenate([s_past, s_new], axis=-1), axis=-1)
    pd = (p[:, :, 0] - lam * p[:, :, 1]).astype(v_new.dtype)
    return (jnp.einsum('bhqk,bkhe->bqhe', pd[..., :past_len], v_past)
            + jnp.einsum('bhqk,bkhe->bqhe', pd[..., past_len:], v_new))


def depthwise_causal_conv(xpad, w, b):
    y = lax.conv_general_dilated(xpad, w[:, None, :].astype(xpad.dtype), (1,), 'VALID',
                                 dimension_numbers=('NWC', 'WIO', 'NWC'),
                                 feature_group_count=CONV_CH)
    return y + b


def decoder_layer(x, c, pos, lp, lambda_init, attend, conv_prev):
    B, T, D = x.shape
    ada = jax.nn.silu(c) @ lp['w_ada'] + lp['b_ada']
    ada = ada.reshape(B, N_SUBLAYERS * N_MOD, D)[:, :, None, :]
    sh1, sc1, g1, sh2, sc2, g2, sh3, sc3, g3 = [ada[:, i] for i in range(N_SUBLAYERS * N_MOD)]
    n1 = modulate(rmsnorm(x, lp['norm1']), sh1, sc1)
    h = x + 0.5 * g1 * swiglu(n1, lp['w1_gate'], lp['w1_up'], lp['w1_down'])
    u = modulate(rmsnorm(h, lp['norm2']), sh2, sc2)
    z = u @ lp['w_in']
    zq, zk, zv, zc, za, zb = jnp.split(z, SPLIT_POINTS, axis=-1)
    q = rope_partial(rmsnorm(zq.reshape(B, T, N_HEADS, 2, HEAD_DIM), lp['q_norm']), pos)
    k = rope_partial(rmsnorm(zk.reshape(B, T, N_HEADS, 2, HEAD_DIM), lp['k_norm']), pos)
    v = zv.reshape(B, T, N_HEADS, V_DIM)
    f32 = jnp.float32
    lam = (jnp.exp(jnp.sum(lp['lam_q1'].astype(f32) * lp['lam_k1'].astype(f32)))
           - jnp.exp(jnp.sum(lp['lam_q2'].astype(f32) * lp['lam_k2'].astype(f32))) + lambda_init)
    o = attend(q, k, v, lam)
    o = rmsnorm(o, lp['subln']) * (1.0 - lambda_init)
    a_out = o.reshape(B, T, ATTN_WIDTH) @ lp['w_attn_out']
    glu = zc[..., :CONV_CH] * jax.nn.sigmoid(zc[..., CONV_CH:])
    padded = jnp.concatenate([conv_prev.astype(glu.dtype), glu], axis=1)
    cv = depthwise_causal_conv(padded, lp['conv_w'], lp['conv_b'])
    cv = jax.nn.silu(layernorm(cv, lp['conv_ln_g'], lp['conv_ln_b']))
    b_out = cv @ lp['w_conv_out'] + lp['b_conv_out']
    m = jax.nn.sigmoid(za) * a_out + jax.nn.sigmoid(zb) * b_out
    h = h + g2 * (m @ lp['w_out'])
    n3 = modulate(rmsnorm(h, lp['norm3']), sh3, sc3)
    y = h + 0.5 * g3 * swiglu(n3, lp['w2_gate'], lp['w2_up'], lp['w2_down'])
    return y, k.reshape(B, T, N_HEADS, 2 * HEAD_DIM), v, padded[:, -CONV_STATE:]


def setup_inputs(seed: int = 0) -> dict:
    key = jax.random.key(seed)
    ks = iter(jax.random.split(key, 48))
    f32 = jnp.float32
    L, D = DEPTH, D_MODEL

    def nrm(shape, scale):
        return jax.random.normal(next(ks), shape, f32) * scale

    def gain(shape):
        return 1.0 + nrm(shape, 0.02)

    n_pages = PAST_LEN // PAGE_SIZE
    n_pool = (DEC_BATCH * n_pages * 5) // 4
    inp = {}
    inp['x_prompt'] = nrm((BATCH, SEQ, D), 1.0)
    inp['x_sample'] = nrm((DEC_BATCH, DEC_SEQ, D), 1.0)
    inp['cache_k'] = nrm((L, n_pool, PAGE_SIZE, N_HEADS, 2 * HEAD_DIM), 1.0)
    inp['cache_v'] = nrm((L, n_pool, PAGE_SIZE, N_HEADS, V_DIM), 1.0)
    inp['state_conv'] = nrm((L, DEC_BATCH, CONV_STATE, CONV_CH), 0.5)
    perm = jax.random.permutation(next(ks), n_pool)
    inp['page_table'] = perm[:DEC_BATCH * n_pages].reshape(DEC_BATCH, n_pages).astype(jnp.int32)
    inp['c_prompt'] = nrm((BATCH, D), 1.0)
    inp['c_sample'] = nrm((DEC_BATCH, D), 1.0)
    inp['w_ada'] = nrm((L, D, N_SUBLAYERS * N_MOD * D), D ** -0.5)
    inp['b_ada'] = nrm((L, N_SUBLAYERS * N_MOD * D), 0.02)
    inp['norm1'] = gain((L, D))
    inp['w1_gate'] = nrm((L, D, D_FF), D ** -0.5)
    inp['w1_up'] = nrm((L, D, D_FF), D ** -0.5)
    inp['w1_down'] = nrm((L, D_FF, D), D_FF ** -0.5)
    inp['norm2'] = gain((L, D))
    inp['w_in'] = nrm((L, D, IN_COLS), D ** -0.5)
    inp['q_norm'] = gain((L, HEAD_DIM))
    inp['k_norm'] = gain((L, HEAD_DIM))
    inp['lam_q1'] = nrm((L, HEAD_DIM), 0.1)
    inp['lam_k1'] = nrm((L, HEAD_DIM), 0.1)
    inp['lam_q2'] = nrm((L, HEAD_DIM), 0.1)
    inp['lam_k2'] = nrm((L, HEAD_DIM), 0.1)
    inp['subln'] = gain((L, V_DIM))
    inp['w_attn_out'] = nrm((L, ATTN_WIDTH, D), ATTN_WIDTH ** -0.5)
    inp['conv_w'] = nrm((L, CONV_WIDTH, CONV_CH), CONV_WIDTH ** -0.5)
    inp['conv_b'] = nrm((L, CONV_CH), 0.02)
    inp['conv_ln_g'] = gain((L, CONV_CH))
    inp['conv_ln_b'] = nrm((L, CONV_CH), 0.02)
    inp['w_conv_out'] = nrm((L, CONV_CH, D), CONV_CH ** -0.5)
    inp['b_conv_out'] = nrm((L, D), 0.02)
    inp['w_out'] = nrm((L, D, D), D ** -0.5)
    inp['norm3'] = gain((L, D))
    inp['w2_gate'] = nrm((L, D, D_FF), D ** -0.5)
    inp['w2_up'] = nrm((L, D, D_FF), D ** -0.5)
    inp['w2_down'] = nrm((L, D_FF, D), D_FF ** -0.5)
    return inp


def reference(x_prompt, x_sample, cache_k, cache_v, state_conv, page_table, c_prompt, c_sample,
              w_ada, b_ada, norm1, w1_gate, w1_up, w1_down, norm2, w_in, q_norm, k_norm,
              lam_q1, lam_k1, lam_q2, lam_k2, subln, w_attn_out, conv_w, conv_b, conv_ln_g,
              conv_ln_b, w_conv_out, b_conv_out, w_out, norm3, w2_gate, w2_up, w2_down):
    B, T = x_prompt.shape[0], x_prompt.shape[1]
    DB, TS = x_sample.shape[0], x_sample.shape[1]
    past_len = page_table.shape[1] * PAGE_SIZE
    pos_prompt = jnp.arange(T, dtype=jnp.int32)
    pos_sample = past_len + jnp.arange(TS, dtype=jnp.int32)
    hp, hs = x_prompt, x_sample
    kp_l, vp_l, cp_l, ks_l, vs_l, cs_l = [], [], [], [], [], []
    for l in range(DEPTH):
        lp = {'w_ada': w_ada[l], 'b_ada': b_ada[l], 'norm1': norm1[l], 'w1_gate': w1_gate[l],
              'w1_up': w1_up[l], 'w1_down': w1_down[l], 'norm2': norm2[l], 'w_in': w_in[l],
              'q_norm': q_norm[l], 'k_norm': k_norm[l], 'lam_q1': lam_q1[l], 'lam_k1': lam_k1[l],
              'lam_q2': lam_q2[l], 'lam_k2': lam_k2[l], 'subln': subln[l],
              'w_attn_out': w_attn_out[l], 'conv_w': conv_w[l], 'conv_b': conv_b[l],
              'conv_ln_g': conv_ln_g[l], 'conv_ln_b': conv_ln_b[l], 'w_conv_out': w_conv_out[l],
              'b_conv_out': b_conv_out[l], 'w_out': w_out[l], 'norm3': norm3[l],
              'w2_gate': w2_gate[l], 'w2_up': w2_up[l], 'w2_down': w2_down[l]}
        lam_init = lambda_init_of(l)
        conv0 = jnp.zeros((B, CONV_STATE, CONV_CH), hp.dtype)
        hp, kp, vp, cp = decoder_layer(
            hp, c_prompt, pos_prompt, lp, lam_init,
            lambda q, k, v, lam: prompt_diff_attention(q, k, v, lam, pos_prompt), conv0)
        k_past = cache_k[l][page_table].reshape(DB, past_len, N_HEADS, 2, HEAD_DIM).astype(hs.dtype)
        v_past = cache_v[l][page_table].reshape(DB, past_len, N_HEADS, V_DIM).astype(hs.dtype)
        hs, ks_, vs_, cs_ = decoder_layer(
            hs, c_sample, pos_sample, lp, lam_init,
            lambda q, k, v, lam: sample_diff_attention(q, k, v, lam, k_past, v_past), state_conv[l])
        kp_l.append(kp); vp_l.append(vp); cp_l.append(cp)
        ks_l.append(ks_); vs_l.append(vs_); cs_l.append(cs_)
    k_prompt = jnp.stack(kp_l)
    v_prompt = jnp.stack(vp_l)
    conv_prompt = jnp.stack(cp_l)
    k_sample = jnp.stack(ks_l)
    v_sample = jnp.stack(vs_l)
    conv_sample = jnp.stack(cs_l)
    return (hp, hs, k_prompt, v_prompt, conv_prompt, k_sample, v_sample, conv_sample)
```

```python
import functools
import math

import jax
import jax.numpy as jnp
from jax import lax
from jax.experimental import pallas as pl
from jax.experimental.pallas import tpu as pltpu

N_HEADS = 8
HEAD_DIM = 64
V_DIM = 2 * HEAD_DIM
QK_COLS = N_HEADS * 2 * HEAD_DIM
ATTN_WIDTH = N_HEADS * V_DIM
ROT_DIM = HEAD_DIM // 4
ROPE_THETA = 500000.0
PAGE_SIZE = 128
CONV_WIDTH = 31
CONV_STATE = CONV_WIDTH - 1
EPS = 1e-6
NEG_INF = -1e30
N_MOD = 9

LANES = 128
SUBLANES = 8
MXU_DIM = 256
VMEM_BYTES_V7X = 64 * 1024 * 1024
VMEM_LIMIT_CAP = VMEM_BYTES_V7X - 6 * 1024 * 1024

CONV_HALO = 32
CONV_ROW_CHUNK = 128

F32 = jnp.float32
BF16 = jnp.bfloat16


def _lambda_init(layer_idx):
    return 0.8 - 0.6 * math.exp(-0.3 * layer_idx)


def _vmem_limit(block_bytes):
    return int(min(VMEM_LIMIT_CAP, block_bytes + block_bytes // 4 + (4 << 20)))


def _nbytes(shape, dtype):
    return math.prod(shape) * jnp.dtype(dtype).itemsize


def _dot(a, b):
    return jnp.dot(a, b, preferred_element_type=F32)


def _sigmoid(x):
    return 1.0 / (1.0 + jnp.exp(-x))


def _rmsnorm_mod(x, gain, shift, scale):
    ms = jnp.mean(x * x, axis=-1, keepdims=True)
    n = x * lax.rsqrt(ms + EPS) * gain
    return n * (1.0 + scale) + shift


def _ada_kernel(c_ref, w_ref, b_ref, o_ref):
    c = c_ref[...]
    a = (c * _sigmoid(c)).astype(BF16)
    o_ref[...] = _dot(a, w_ref[...].astype(BF16)) + b_ref[...]


def _ada(c, w, b):
    m, d = c.shape
    n = w.shape[1]
    tn = d // 2
    assert n % tn == 0 and tn % LANES == 0
    blocks = 2 * (_nbytes((m, d), F32) + _nbytes((d, tn), F32) + _nbytes((m, tn), F32))
    return pl.pallas_call(
        _ada_kernel,
        out_shape=jax.ShapeDtypeStruct((m, n), F32),
        grid=(n // tn,),
        in_specs=[pl.BlockSpec((m, d), lambda j: (0, 0)),
                  pl.BlockSpec((d, tn), lambda j: (0, j)),
                  pl.BlockSpec((1, tn), lambda j: (0, j))],
        out_specs=pl.BlockSpec((m, tn), lambda j: (0, j)),
        compiler_params=pltpu.CompilerParams(dimension_semantics=("arbitrary",),
                                             vmem_limit_bytes=_vmem_limit(blocks)),
        name="ada",
    )(c, w, b)


def _ffn_kernel(x_ref, ng_ref, sh_ref, sc_ref, g_ref, wg_ref, wu_ref, wd_ref, o_ref, n_ref, acc_ref):
    j = pl.program_id(1)

    @pl.when(j == 0)
    def _():
        n = _rmsnorm_mod(x_ref[...], ng_ref[...], sh_ref[...], sc_ref[...])
        n_ref[...] = n.astype(BF16)
        acc_ref[...] = jnp.zeros_like(acc_ref)

    n = n_ref[...]
    a = _dot(n, wg_ref[...])
    u = _dot(n, wu_ref[...])
    hmid = (a * _sigmoid(a) * u).astype(BF16)
    acc_ref[...] += _dot(hmid, wd_ref[...])

    @pl.when(j == pl.num_programs(1) - 1)
    def _():
        o_ref[...] = x_ref[...] + 0.5 * g_ref[...] * acc_ref[...]


def _mod_spec(mod, tm, d):
    if mod.shape[0] == 1:
        return pl.BlockSpec((1, d), lambda i, j: (0, 0))
    return pl.BlockSpec((tm, d), lambda i, j: (i, 0))


def _ffn(x, norm_g, shift, scale, gate, wg, wu, wd, *, tm, tf):
    m, d = x.shape
    fp = wg.shape[1]
    tm = min(tm, m)
    assert m % tm == 0 and fp % tf == 0
    blocks = (2 * (2 * _nbytes((tm, d), F32) + 2 * _nbytes((d, tf), BF16) + _nbytes((tf, d), BF16))
              + _nbytes((tm, d), BF16) + _nbytes((tm, d), F32) + 2 * 3 * _nbytes((min(shift.shape[0], tm), d), F32))
    return pl.pallas_call(
        _ffn_kernel,
        out_shape=jax.ShapeDtypeStruct((m, d), F32),
        grid=(m // tm, fp // tf),
        in_specs=[pl.BlockSpec((tm, d), lambda i, j: (i, 0)),
                  pl.BlockSpec((1, d), lambda i, j: (0, 0)),
                  _mod_spec(shift, tm, d), _mod_spec(scale, tm, d), _mod_spec(gate, tm, d),
                  pl.BlockSpec((d, tf), lambda i, j: (0, j)),
                  pl.BlockSpec((d, tf), lambda i, j: (0, j)),
                  pl.BlockSpec((tf, d), lambda i, j: (j, 0))],
        out_specs=pl.BlockSpec((tm, d), lambda i, j: (i, 0)),
        scratch_shapes=[pltpu.VMEM((tm, d), BF16), pltpu.VMEM((tm, d), F32)],
        compiler_params=pltpu.CompilerParams(dimension_semantics=("parallel", "arbitrary"),
                                             vmem_limit_bytes=_vmem_limit(blocks)),
        name="ffn",
    )(x, norm_g, shift, scale, gate, wg, wu, wd)


def _subhead_rms_rope(z, gain, grp, cos_t, sin_a, sin_b):
    tn = z.shape[1]
    zz = (z * z).astype(BF16)
    ms = jnp.concatenate([_dot(zz[:, c:c + MXU_DIM], grp) for c in range(0, tn, MXU_DIM)], axis=1)
    y = z * lax.rsqrt(ms + EPS) * gain
    outs = []
    for c in range(0, tn, LANES):
        yc = y[:, c:c + LANES]
        outs.append(yc * cos_t + pltpu.roll(yc, LANES - ROT_DIM // 2, 1) * sin_a
                    + pltpu.roll(yc, ROT_DIM // 2, 1) * sin_b)
    return jnp.concatenate(outs, axis=1)


def _mix_in_kernel(h_ref, ng_ref, sh_ref, sc_ref, w_ref, qg_ref, kg_ref, grp_ref, cos_ref, sa_ref, sb_ref,
                   q_ref, k_ref, kb_ref, v_ref, vb_ref, glu_ref, ga_ref, gb_ref, u_ref, zc_ref, *, nq):
    j = pl.program_id(1)

    @pl.when(j == 0)
    def _():
        u_ref[...] = _rmsnorm_mod(h_ref[...], ng_ref[...], sh_ref[...], sc_ref[...]).astype(BF16)

    z = _dot(u_ref[...], w_ref[...])

    @pl.when(j < nq)
    def _():
        r = _subhead_rms_rope(z, qg_ref[...], grp_ref[...], cos_ref[...], sa_ref[...], sb_ref[...])
        q_ref[...] = (r * (HEAD_DIM ** -0.5)).astype(BF16)

    @pl.when((j >= nq) & (j < 2 * nq))
    def _():
        r = _subhead_rms_rope(z, kg_ref[...], grp_ref[...], cos_ref[...], sa_ref[...], sb_ref[...])
        k_ref[...] = r
        kb_ref[...] = r.astype(BF16)

    @pl.when((j >= 2 * nq) & (j < 3 * nq))
    def _():
        v_ref[...] = z
        vb_ref[...] = z.astype(BF16)

    @pl.when(j == 3 * nq)
    def _():
        zc_ref[...] = z

    @pl.when(j == 3 * nq + 1)
    def _():
        glu_ref[...] = zc_ref[...] * _sigmoid(z)

    @pl.when((j >= 3 * nq + 2) & (j < 3 * nq + 4))
    def _():
        ga_ref[...] = _sigmoid(z)

    @pl.when(j >= 3 * nq + 4)
    def _():
        gb_ref[...] = _sigmoid(z)


def _mix_in(h, norm_g, shift, scale, w_in, q_gain, k_gain, grp, cos_t, sin_a, sin_b, *, tm):
    m, d = h.shape
    tn = d // 2
    nq = QK_COLS // tn
    assert QK_COLS % tn == 0 and tn % LANES == 0
    ncol = w_in.shape[1] // tn
    assert ncol == 3 * nq + 6
    tm = min(tm, m)
    assert m % tm == 0

    def col(lo):
        return lambda i, j: (i, jnp.clip(j - lo, 0, nq - 1))

    def col2(lo):
        return lambda i, j: (i, jnp.clip(j - lo, 0, 1))

    row = lambda i, j: (i, 0)
    const = lambda i, j: (0, 0)
    blocks = (2 * (_nbytes((tm, d), F32) + _nbytes((d, tn), BF16) + 3 * _nbytes((tm, LANES), F32)
                   + 6 * _nbytes((tm, tn), F32) + 3 * _nbytes((tm, tn), BF16))
              + _nbytes((tm, d), BF16) + _nbytes((tm, tn), F32) + 2 * 2 * _nbytes((min(shift.shape[0], tm), d), F32))
    gsz = grp.shape[0]
    outs = pl.pallas_call(
        functools.partial(_mix_in_kernel, nq=nq),
        out_shape=[jax.ShapeDtypeStruct((m, QK_COLS), BF16),
                   jax.ShapeDtypeStruct((m, QK_COLS), F32),
                   jax.ShapeDtypeStruct((m, QK_COLS), BF16),
                   jax.ShapeDtypeStruct((m, ATTN_WIDTH), F32),
                   jax.ShapeDtypeStruct((m, ATTN_WIDTH), BF16),
                   jax.ShapeDtypeStruct((m, tn), F32),
                   jax.ShapeDtypeStruct((m, d), F32),
                   jax.ShapeDtypeStruct((m, d), F32)],
        grid=(m // tm, ncol),
        in_specs=[pl.BlockSpec((tm, d), row),
                  pl.BlockSpec((1, d), const),
                  _mod_spec(shift, tm, d), _mod_spec(scale, tm, d),
                  pl.BlockSpec((d, tn), lambda i, j: (0, j)),
                  pl.BlockSpec((1, tn), const), pl.BlockSpec((1, tn), const),
                  pl.BlockSpec((gsz, gsz), const),
                  pl.BlockSpec((tm, LANES), row), pl.BlockSpec((tm, LANES), row), pl.BlockSpec((tm, LANES), row)],
        out_specs=[pl.BlockSpec((tm, tn), col(0)),
                   pl.BlockSpec((tm, tn), col(nq)), pl.BlockSpec((tm, tn), col(nq)),
                   pl.BlockSpec((tm, tn), col(2 * nq)), pl.BlockSpec((tm, tn), col(2 * nq)),
                   pl.BlockSpec((tm, tn), row),
                   pl.BlockSpec((tm, tn), col2(3 * nq + 2)),
                   pl.BlockSpec((tm, tn), col2(3 * nq + 4))],
        scratch_shapes=[pltpu.VMEM((tm, d), BF16), pltpu.VMEM((tm, tn), F32)],
        compiler_params=pltpu.CompilerParams(dimension_semantics=("parallel", "arbitrary"),
                                             vmem_limit_bytes=_vmem_limit(blocks)),
        name="mix_in",
    )(h, norm_g, shift, scale, w_in, q_gain, k_gain, grp, cos_t, sin_a, sin_b)
    return outs


def _lambda_value(lam_ref, lam_init):
    lv = lam_ref[...]
    l1 = jnp.sum(lv[0:1] * lv[1:2], axis=-1, keepdims=True)
    l2 = jnp.sum(lv[2:3] * lv[3:4], axis=-1, keepdims=True)
    return jnp.exp(l1) - jnp.exp(l2) + lam_init


def _subln(o, gain, lam_init):
    ms = jnp.mean(o * o, axis=-1, keepdims=True)
    return o * lax.rsqrt(ms + EPS) * gain * (1.0 - lam_init)


def _attn_kernel(qi_ref, kj_ref, lam_ref, sg_ref, q_ref, k_ref, v_ref, o_ref,
                 q0_ref, q1_ref, m_ref, l_ref, acc_ref, *, tq, tk, lam_init):
    s_idx = pl.program_id(1)
    qi = qi_ref[s_idx]
    kj = kj_ref[s_idx]
    q_lo = qi * tq
    k_lo = kj * tk
    last_kj = (q_lo + tq - 1) // tk

    @pl.when(kj == 0)
    def _():
        q = q_ref[...]
        lane = lax.broadcasted_iota(jnp.int32, q.shape, 1)
        zero = jnp.zeros_like(q)
        q0_ref[...] = jnp.where(lane < HEAD_DIM, q, zero)
        q1_ref[...] = jnp.where(lane >= HEAD_DIM, q, zero)
        m_ref[...] = jnp.full_like(m_ref, NEG_INF)
        l_ref[...] = jnp.zeros_like(l_ref)
        acc_ref[...] = jnp.zeros_like(acc_ref)

    def step(masked):
        k = k_ref[...]
        v = v_ref[...]
        if masked:
            rows = q_lo + lax.broadcasted_iota(jnp.int32, (tq, tk), 0)
            cols = k_lo + lax.broadcasted_iota(jnp.int32, (tq, tk), 1)
            keep = rows >= cols
        for sub, qs_ref in enumerate((q0_ref, q1_ref)):
            s = lax.dot_general(qs_ref[...], k, (((1,), (1,)), ((), ())), preferred_element_type=F32)
            if masked:
                s = jnp.where(keep, s, NEG_INF)
            m_old = m_ref[sub]
            m_new = jnp.maximum(m_old, jnp.max(s, axis=-1, keepdims=True))
            alpha = jnp.exp(m_old - m_new)
            p = jnp.exp(s - m_new)
            l_ref[sub] = alpha * l_ref[sub] + jnp.sum(p, axis=-1, keepdims=True)
            acc_ref[sub] = alpha * acc_ref[sub] + _dot(p.astype(BF16), v)
            m_ref[sub] = m_new

    needs_mask = k_lo + tk - 1 > q_lo

    @pl.when(needs_mask)
    def _():
        step(True)

    @pl.when(jnp.logical_not(needs_mask))
    def _():
        step(False)

    @pl.when(kj == last_kj)
    def _():
        lam = _lambda_value(lam_ref, lam_init)
        o = acc_ref[0] / l_ref[0] - lam * (acc_ref[1] / l_ref[1])
        o_ref[...] = _subln(o, sg_ref[...], lam_init).astype(o_ref.dtype)


def _prompt_attention(q, k, v, lam_vecs, subln_g, *, tq, tk, lam_init):
    t = q.shape[0]
    tq = min(tq, t)
    tk = min(tk, t)
    assert t % tq == 0 and t % tk == 0
    steps = [(a, b) for a in range(t // tq) for b in range((a * tq + tq - 1) // tk + 1)]
    qi_arr = jnp.asarray([a for a, _ in steps], jnp.int32)
    kj_arr = jnp.asarray([b for _, b in steps], jnp.int32)
    blocks = (2 * (2 * _nbytes((tq, V_DIM), BF16) + 2 * _nbytes((tk, V_DIM), BF16))
              + 2 * _nbytes((tq, V_DIM), BF16) + 4 * _nbytes((tq, LANES), F32) + 2 * _nbytes((tq, V_DIM), F32)
              + 6 * _nbytes((tq, tk), F32))
    grid_spec = pltpu.PrefetchScalarGridSpec(
        num_scalar_prefetch=2,
        grid=(N_HEADS, len(steps)),
        in_specs=[pl.BlockSpec((4, HEAD_DIM), lambda h, s, qi, kj: (0, 0)),
                  pl.BlockSpec((1, V_DIM), lambda h, s, qi, kj: (0, 0)),
                  pl.BlockSpec((tq, V_DIM), lambda h, s, qi, kj: (qi[s], h)),
                  pl.BlockSpec((tk, V_DIM), lambda h, s, qi, kj: (kj[s], h)),
                  pl.BlockSpec((tk, V_DIM), lambda h, s, qi, kj: (kj[s], h))],
        out_specs=pl.BlockSpec((tq, V_DIM), lambda h, s, qi, kj: (qi[s], h)),
        scratch_shapes=[pltpu.VMEM((tq, V_DIM), BF16), pltpu.VMEM((tq, V_DIM), BF16),
                        pltpu.VMEM((2, tq, 1), F32), pltpu.VMEM((2, tq, 1), F32),
                        pltpu.VMEM((2, tq, V_DIM), F32)])
    return pl.pallas_call(
        functools.partial(_attn_kernel, tq=tq, tk=tk, lam_init=lam_init),
        out_shape=jax.ShapeDtypeStruct((t, ATTN_WIDTH), BF16),
        grid_spec=grid_spec,
        compiler_params=pltpu.CompilerParams(dimension_semantics=("parallel", "arbitrary"),
                                             vmem_limit_bytes=_vmem_limit(blocks)),
        name="prompt_attn",
    )(qi_arr, kj_arr, lam_vecs, subln_g, q, k, v)


def _decode_kernel(pt_ref, lam_ref, sg_ref, red_ref, q_ref, kn_ref, vn_ref, kp_ref, vp_ref, o_ref,
                   m_ref, l_ref, acc_ref, *, lam_init):
    p_idx = pl.program_id(1)

    @pl.when(p_idx == 0)
    def _():
        m_ref[...] = jnp.full_like(m_ref, NEG_INF)
        l_ref[...] = jnp.zeros_like(l_ref)
        acc_ref[...] = jnp.zeros_like(acc_ref)

    q = q_ref[0]

    def absorb(kblk, vblk):
        n = kblk.shape[0]
        prod = (kblk * q[None]).reshape(n * N_HEADS, V_DIM).astype(BF16)
        s_all = _dot(prod, red_ref[...])
        for sub in range(2):
            s = s_all[:, sub * LANES:(sub + 1) * LANES].reshape(n, N_HEADS, LANES)
            m_old = m_ref[sub]
            m_new = jnp.maximum(m_old, jnp.max(s, axis=0))
            alpha = jnp.exp(m_old - m_new)
            p = jnp.exp(s - m_new[None])
            l_ref[sub] = alpha * l_ref[sub] + jnp.sum(p, axis=0)
            acc_ref[sub] = alpha * acc_ref[sub] + jnp.sum(p * vblk, axis=0)
            m_ref[sub] = m_new

    absorb(kp_ref[...], vp_ref[...])

    @pl.when(p_idx == pl.num_programs(1) - 1)
    def _():
        kn = jnp.broadcast_to(kn_ref[...], (SUBLANES, N_HEADS, V_DIM))
        vn = jnp.broadcast_to(vn_ref[...], (SUBLANES, N_HEADS, V_DIM))
        prod = (kn * q[None]).reshape(SUBLANES * N_HEADS, V_DIM).astype(BF16)
        s_all = _dot(prod, red_ref[...])
        lam = _lambda_value(lam_ref, lam_init)
        outs = []
        for sub in range(2):
            s = s_all[:, sub * LANES:(sub + 1) * LANES].reshape(SUBLANES, N_HEADS, LANES)[0]
            m_old = m_ref[sub]
            m_new = jnp.maximum(m_old, s)
            alpha = jnp.exp(m_old - m_new)
            p = jnp.exp(s - m_new)
            l_fin = alpha * l_ref[sub] + p
            acc_fin = alpha * acc_ref[sub] + p * vn[0]
            outs.append(acc_fin / l_fin)
        o = outs[0] - lam * outs[1]
        o_ref[0] = _subln(o, sg_ref[...], lam_init)


def _decode_attention(page_table, q, k_new, v_new, cache_k, cache_v, lam_vecs, subln_g, red, *, lam_init):
    db, n_pages = page_table.shape
    pt_flat = page_table.reshape(-1)
    blk = (1, N_HEADS, V_DIM)
    page_blk = (None, PAGE_SIZE, N_HEADS, V_DIM)
    blocks = 2 * (2 * _nbytes((PAGE_SIZE, N_HEADS, V_DIM), F32)) + 12 * _nbytes((PAGE_SIZE, N_HEADS, V_DIM), F32)
    grid_spec = pltpu.PrefetchScalarGridSpec(
        num_scalar_prefetch=1,
        grid=(db, n_pages),
        in_specs=[pl.BlockSpec((4, HEAD_DIM), lambda b, p, pt: (0, 0)),
                  pl.BlockSpec((1, V_DIM), lambda b, p, pt: (0, 0)),
                  pl.BlockSpec((V_DIM, 2 * LANES), lambda b, p, pt: (0, 0)),
                  pl.BlockSpec(blk, lambda b, p, pt: (b, 0, 0)),
                  pl.BlockSpec(blk, lambda b, p, pt: (b, 0, 0)),
                  pl.BlockSpec(blk, lambda b, p, pt: (b, 0, 0)),
                  pl.BlockSpec(page_blk, lambda b, p, pt: (pt[b * n_pages + p], 0, 0, 0)),
                  pl.BlockSpec(page_blk, lambda b, p, pt: (pt[b * n_pages + p], 0, 0, 0))],
        out_specs=pl.BlockSpec(blk, lambda b, p, pt: (b, 0, 0)),
        scratch_shapes=[pltpu.VMEM((2, N_HEADS, LANES), F32), pltpu.VMEM((2, N_HEADS, LANES), F32),
                        pltpu.VMEM((2, N_HEADS, V_DIM), F32)])
    return pl.pallas_call(
        functools.partial(_decode_kernel, lam_init=lam_init),
        out_shape=jax.ShapeDtypeStruct((db, N_HEADS, V_DIM), F32),
        grid_spec=grid_spec,
        compiler_params=pltpu.CompilerParams(dimension_semantics=("parallel", "arbitrary"),
                                             vmem_limit_bytes=_vmem_limit(blocks)),
        name="decode_attn",
    )(pt_flat, lam_vecs, subln_g, red, q, k_new, v_new, cache_k, cache_v)


def _mix_out_tail(y, cb_ref, lng_ref, lnb_ref, wco_ref, bco_ref, o_ref, wao_ref, ga_ref, gb_ref, h_ref, g_ref,
                  wout_ref, out_ref):
    y = y + cb_ref[...]
    mu = jnp.mean(y, axis=-1, keepdims=True)
    yc = y - mu
    var = jnp.mean(yc * yc, axis=-1, keepdims=True)
    cv = yc * lax.rsqrt(var + EPS) * lng_ref[...] + lnb_ref[...]
    cv = (cv * _sigmoid(cv)).astype(BF16)
    b_out = _dot(cv, wco_ref[...]) + bco_ref[...]
    a_out = _dot(o_ref[...].astype(BF16), wao_ref[...])
    mrg = (ga_ref[...] * a_out + gb_ref[...] * b_out).astype(BF16)
    out_ref[...] = h_ref[...] + g_ref[...] * _dot(mrg, wout_ref[...])


def _mix_out_prompt_kernel(glu_ref, halo_ref, cw_ref, cb_ref, lng_ref, lnb_ref, wco_ref, bco_ref, o_ref, wao_ref,
                           ga_ref, gb_ref, h_ref, g_ref, wout_ref, out_ref, ext_ref, y_ref, *, tm):
    i = pl.program_id(0)
    c = glu_ref.shape[1]

    @pl.when(i == 0)
    def _():
        ext_ref[0:CONV_HALO, :] = jnp.zeros((CONV_HALO, c), F32)

    @pl.when(i > 0)
    def _():
        ext_ref[0:CONV_HALO, :] = halo_ref[...]

    ext_ref[CONV_HALO:, :] = glu_ref[...]
    rc = min(CONV_ROW_CHUNK, tm)
    off = CONV_HALO - CONV_STATE
    for r0 in range(0, tm, rc):
        for c0 in range(0, c, LANES):
            acc = jnp.zeros((rc, LANES), F32)
            for j in range(CONV_WIDTH):
                acc = acc + cw_ref[j:j + 1, c0:c0 + LANES] * ext_ref[r0 + off + j:r0 + off + j + rc, c0:c0 + LANES]
            y_ref[r0:r0 + rc, c0:c0 + LANES] = acc
    _mix_out_tail(y_ref[...], cb_ref, lng_ref, lnb_ref, wco_ref, bco_ref, o_ref, wao_ref, ga_ref, gb_ref, h_ref,
                  g_ref, wout_ref, out_ref)


def _mix_out_sample_kernel(win_ref, cw_ref, cb_ref, lng_ref, lnb_ref, wco_ref, bco_ref, o_ref, wao_ref,
                           ga_ref, gb_ref, h_ref, g_ref, wout_ref, out_ref):
    y = cw_ref[0:1, :] * win_ref[0]
    for j in range(1, CONV_WIDTH):
        y = y + cw_ref[j:j + 1, :] * win_ref[j]
    _mix_out_tail(y, cb_ref, lng_ref, lnb_ref, wco_ref, bco_ref, o_ref, wao_ref, ga_ref, gb_ref, h_ref, g_ref,
                  wout_ref, out_ref)


def _resident(shape):
    return pl.BlockSpec(shape, lambda i: (0,) * len(shape), pipeline_mode=pl.Buffered(1))


def _mix_out_common_specs(tm, c, d, aw, gate):
    row = lambda i: (i, 0)
    g_spec = (pl.BlockSpec((1, d), lambda i: (0, 0)) if gate.shape[0] == 1 else pl.BlockSpec((tm, d), row))
    return [_resident((CONV_HALO, c)), _resident((1, c)), _resident((1, c)), _resident((1, c)),
            _resident((c, d)), _resident((1, d)),
            pl.BlockSpec((tm, aw), row), _resident((aw, d)),
            pl.BlockSpec((tm, d), row), pl.BlockSpec((tm, d), row), pl.BlockSpec((tm, d), row), g_spec,
            _resident((d, d))]


def _mix_out_bytes(tm, c, d, aw, o_dtype):
    return (2 * (_nbytes((tm, aw), o_dtype) + 4 * _nbytes((tm, d), F32) + _nbytes((tm, d), F32))
            + _nbytes((c, d), BF16) + _nbytes((aw, d), BF16) + _nbytes((d, d), BF16)
            + 8 * _nbytes((tm, d), F32))


def _mix_out_prompt(glu, conv_w, conv_b, ln_g, ln_b, w_co, b_co, o, w_ao, ga, gb, h, gate, w_out, *, tm):
    m, c = glu.shape
    d = h.shape[1]
    aw = o.shape[1]
    tm = min(tm, m)
    assert m % tm == 0 and tm % CONV_HALO == 0
    hb = tm // CONV_HALO
    blocks = _mix_out_bytes(tm, c, d, aw, o.dtype) + 4 * _nbytes((tm + CONV_HALO, c), F32)
    return pl.pallas_call(
        functools.partial(_mix_out_prompt_kernel, tm=tm),
        out_shape=jax.ShapeDtypeStruct((m, d), F32),
        grid=(m // tm,),
        in_specs=[pl.BlockSpec((tm, c), lambda i: (i, 0)),
                  pl.BlockSpec((CONV_HALO, c), lambda i: (jnp.maximum(i * hb - 1, 0), 0))]
                 + _mix_out_common_specs(tm, c, d, aw, gate),
        out_specs=pl.BlockSpec((tm, d), lambda i: (i, 0)),
        scratch_shapes=[pltpu.VMEM((tm + CONV_HALO, c), F32), pltpu.VMEM((tm, c), F32)],
        compiler_params=pltpu.CompilerParams(dimension_semantics=("arbitrary",),
                                             vmem_limit_bytes=_vmem_limit(blocks)),
        name="mix_out_prompt",
    )(glu, glu, conv_w, conv_b, ln_g, ln_b, w_co, b_co, o, w_ao, ga, gb, h, gate, w_out)


def _mix_out_sample(win, conv_w, conv_b, ln_g, ln_b, w_co, b_co, o, w_ao, ga, gb, h, gate, w_out):
    _, m, c = win.shape
    d = h.shape[1]
    aw = o.shape[1]
    blocks = _mix_out_bytes(m, c, d, aw, o.dtype) + 2 * _nbytes(win.shape, F32)
    return pl.pallas_call(
        _mix_out_sample_kernel,
        out_shape=jax.ShapeDtypeStruct((m, d), F32),
        grid=(1,),
        in_specs=[pl.BlockSpec(win.shape, lambda i: (0, 0, 0))] + _mix_out_common_specs(m, c, d, aw, gate),
        out_specs=pl.BlockSpec((m, d), lambda i: (0, 0)),
        compiler_params=pltpu.CompilerParams(dimension_semantics=("arbitrary",),
                                             vmem_limit_bytes=_vmem_limit(blocks)),
        name="mix_out_sample",
    )(win, conv_w, conv_b, ln_g, ln_b, w_co, b_co, o, w_ao, ga, gb, h, gate, w_out)


def _rope_tables(pos):
    half = ROT_DIM // 2
    inv = jnp.power(ROPE_THETA, -jnp.arange(half, dtype=F32) * 2.0 / ROT_DIM)
    ang = pos.astype(F32)[:, None] * inv[None, :]
    cos, sin = jnp.cos(ang), jnp.sin(ang)
    t = pos.shape[0]
    pad = jnp.zeros((t, HEAD_DIM - ROT_DIM), F32)
    zero = jnp.zeros((t, half), F32)
    cos_t = jnp.concatenate([cos, cos, pad + 1.0], axis=1)
    sin_a = jnp.concatenate([-sin, zero, pad], axis=1)
    sin_b = jnp.concatenate([zero, sin, pad], axis=1)
    rep = LANES // HEAD_DIM
    return jnp.tile(cos_t, (1, rep)), jnp.tile(sin_a, (1, rep)), jnp.tile(sin_b, (1, rep))


def _group_mean_matrix(n):
    g = jnp.arange(n) // HEAD_DIM
    return jnp.where(g[:, None] == g[None, :], 1.0 / HEAD_DIM, 0.0).astype(BF16)


def _half_sum_matrix():
    lane_half = jnp.arange(V_DIM) // HEAD_DIM
    col_half = jnp.arange(2 * LANES) // LANES
    return (lane_half[:, None] == col_half[None, :]).astype(BF16)


def _pad_cols(w, mult):
    return jnp.pad(w, ((0, 0), (0, -w.shape[1] % mult)))


def _pad_rows(w, mult):
    return jnp.pad(w, ((0, -w.shape[0] % mult), (0, 0)))


def _tile_config(seq, d_ff):
    return dict(ffn_tm=min(512, seq), ffn_tf=min(512, pl.cdiv(d_ff, LANES) * LANES), mix_in_tm=min(512, seq),
                attn_tq=min(512, seq), attn_tk=min(512, seq), mix_out_tm=min(256, seq))


def kernel(x_prompt, x_sample, cache_k, cache_v, state_conv, page_table, c_prompt, c_sample, w_ada, b_ada, norm1, w1_gate, w1_up, w1_down, norm2, w_in, q_norm, k_norm, lam_q1, lam_k1, lam_q2, lam_k2, subln, w_attn_out, conv_w, conv_b, conv_ln_g, conv_ln_b, w_conv_out, b_conv_out, w_out, norm3, w2_gate, w2_up, w2_down):
    bsz, seq, d = x_prompt.shape
    db, ts, _ = x_sample.shape
    depth = w_ada.shape[0]
    assert bsz == 1 and ts == 1 and depth == 1, "kernel is specialised to one prompt sequence, one new token, one layer"
    c = conv_w.shape[-1]
    d_ff = w1_gate.shape[-1]
    past_len = page_table.shape[1] * PAGE_SIZE
    cfg = _tile_config(seq, d_ff)
    lam_init = _lambda_init(0)

    tf = cfg["ffn_tf"]
    w1g = _pad_cols(w1_gate[0].astype(BF16), tf)
    w1u = _pad_cols(w1_up[0].astype(BF16), tf)
    w1d = _pad_rows(w1_down[0].astype(BF16), tf)
    w2g = _pad_cols(w2_gate[0].astype(BF16), tf)
    w2u = _pad_cols(w2_up[0].astype(BF16), tf)
    w2d = _pad_rows(w2_down[0].astype(BF16), tf)
    w_in_b = w_in[0].astype(BF16)
    w_ao = w_attn_out[0].astype(BF16)
    w_co = w_conv_out[0].astype(BF16)
    w_o = w_out[0].astype(BF16)

    n_c = bsz + db
    c_all = jnp.concatenate([c_prompt, c_sample, jnp.zeros((-n_c % SUBLANES, d), F32)], axis=0)
    ada = _ada(c_all, w_ada[0], b_ada)
    ada_p = [ada[0:1, i * d:(i + 1) * d] for i in range(N_MOD)]
    ada_s = [ada[1:1 + db, i * d:(i + 1) * d] for i in range(N_MOD)]

    qg = jnp.tile(q_norm[0], c // HEAD_DIM)[None]
    kg = jnp.tile(k_norm[0], c // HEAD_DIM)[None]
    grp = _group_mean_matrix(min(MXU_DIM, c))
    red = _half_sum_matrix()
    lam_vecs = jnp.stack([lam_q1[0], lam_k1[0], lam_q2[0], lam_k2[0]])
    cw = jnp.pad(conv_w[0], ((0, CONV_HALO - CONV_WIDTH), (0, 0)))

    def layer_front(x, mods, pos, tm_ffn, tm_mix):
        sh1, sc1, g1, sh2, sc2 = mods[:5]
        h = _ffn(x, norm1, sh1, sc1, g1, w1g, w1u, w1d, tm=tm_ffn, tf=tf)
        cos_t, sin_a, sin_b = _rope_tables(pos)
        return (h,) + tuple(_mix_in(h, norm2, sh2, sc2, w_in_b, qg, kg, grp, cos_t, sin_a, sin_b, tm=tm_mix))

    def layer_back(h, mods, tm_ffn):
        sh3, sc3, g3 = mods[6:9]
        return _ffn(h, norm3, sh3, sc3, g3, w2g, w2u, w2d, tm=tm_ffn, tf=tf)

    xp = x_prompt[0]
    pos_p = jnp.arange(seq, dtype=jnp.int32)
    hp, qp, kp, kpb, vp, vpb, glu_p, ga_p, gb_p = layer_front(xp, ada_p, pos_p, cfg["ffn_tm"], cfg["mix_in_tm"])
    op = _prompt_attention(qp, kpb, vpb, lam_vecs, subln, tq=cfg["attn_tq"], tk=cfg["attn_tk"], lam_init=lam_init)
    hp2 = _mix_out_prompt(glu_p, cw, conv_b, conv_ln_g, conv_ln_b, w_co, b_conv_out, op, w_ao, ga_p, gb_p, hp,
                          ada_p[5], w_o, tm=cfg["mix_out_tm"])
    yp = layer_back(hp2, ada_p, cfg["ffn_tm"])

    xs = x_sample[:, 0]
    pos_s = jnp.full((db,), past_len, jnp.int32)
    hs, qs, ks, _, vs, _, glu_s, ga_s, gb_s = layer_front(xs, ada_s, pos_s, db, db)
    os_ = _decode_attention(page_table, qs.astype(F32).reshape(db, N_HEADS, V_DIM), ks.reshape(db, N_HEADS, V_DIM),
                            vs.reshape(db, N_HEADS, V_DIM), cache_k.reshape(cache_k.shape[1:]),
                            cache_v.reshape(cache_v.shape[1:]), lam_vecs, subln, red,
                            lam_init=lam_init)
    padded_s = jnp.concatenate([state_conv[0], glu_s[:, None, :]], axis=1)
    win = jnp.pad(padded_s.transpose(1, 0, 2), ((0, CONV_HALO - CONV_WIDTH), (0, 0), (0, 0)))
    hs2 = _mix_out_sample(win, cw, conv_b, conv_ln_g, conv_ln_b, w_co, b_conv_out,
                          os_.reshape(db, ATTN_WIDTH), w_ao, ga_s, gb_s, hs, ada_s[5], w_o)
    ys = layer_back(hs2, ada_s, db)

    return (yp[None],
            ys[:, None, :],
            kp.reshape(1, 1, seq, N_HEADS, 2 * HEAD_DIM),
            vp.reshape(1, 1, seq, N_HEADS, V_DIM),
            glu_p[seq - CONV_STATE:][None, None],
            ks.reshape(1, db, 1, N_HEADS, 2 * HEAD_DIM),
            vs.reshape(1, db, 1, N_HEADS, V_DIM),
            padded_s[None, :, 1:, :])
```

```python
import functools
import math

import jax
import jax.numpy as jnp
from jax import lax
from jax.experimental import pallas as pl
from jax.experimental.pallas import tpu as pltpu

N_HEADS = 8
HEAD_DIM = 64
V_DIM = 2 * HEAD_DIM
QK_COLS = N_HEADS * 2 * HEAD_DIM
ATTN_WIDTH = N_HEADS * V_DIM
ROT_DIM = HEAD_DIM // 4
ROPE_THETA = 500000.0
PAGE_SIZE = 128
CONV_WIDTH = 31
CONV_STATE = CONV_WIDTH - 1
EPS = 1e-6
NEG_INF = -1e30
N_MOD = 9
QK_SCALE_LOG2 = HEAD_DIM ** -0.5 * math.log2(math.e)
V_ROWS = V_DIM + 16

LANES = 128
SUBLANES = 8
MXU_DIM = 256
VMEM_BYTES_V7X = 64 * 1024 * 1024
VMEM_LIMIT_CAP = VMEM_BYTES_V7X - 6 * 1024 * 1024

CONV_HALO = 32
CONV_ROW_CHUNK = 128
DECODE_PAGES_PER_STEP = 4
REDUCE_WAYS = 8

F32 = jnp.float32
BF16 = jnp.bfloat16


def _lambda_init(layer_idx):
    return 0.8 - 0.6 * math.exp(-0.3 * layer_idx)


def _vmem_limit(block_bytes):
    return int(min(VMEM_LIMIT_CAP, block_bytes + block_bytes // 4 + (4 << 20)))


def _nbytes(shape, dtype):
    return math.prod(shape) * jnp.dtype(dtype).itemsize


def _dot(a, b):
    return jnp.dot(a, b, preferred_element_type=F32)


def _sigmoid(x):
    return 1.0 / (1.0 + jnp.exp(-x))


def _rmsnorm_mod(x, gain, shift, scale):
    ms = jnp.mean(x * x, axis=-1, keepdims=True)
    n = x * lax.rsqrt(ms + EPS) * gain
    return n * (1.0 + scale) + shift


def _ada_kernel(c_ref, w_ref, b_ref, o_ref):
    c = c_ref[...]
    a = (c * _sigmoid(c)).astype(BF16)
    o_ref[...] = _dot(a, w_ref[...].astype(BF16)) + b_ref[...]


def _ada(c, w, b):
    m, d = c.shape
    n = w.shape[1]
    tn = d // 2
    assert n % tn == 0 and tn % LANES == 0
    blocks = 2 * (_nbytes((m, d), F32) + _nbytes((d, tn), F32) + _nbytes((m, tn), F32))
    return pl.pallas_call(
        _ada_kernel,
        out_shape=jax.ShapeDtypeStruct((m, n), F32),
        grid=(n // tn,),
        in_specs=[pl.BlockSpec((m, d), lambda j: (0, 0)),
                  pl.BlockSpec((d, tn), lambda j: (0, j)),
                  pl.BlockSpec((1, tn), lambda j: (0, j))],
        out_specs=pl.BlockSpec((m, tn), lambda j: (0, j)),
        compiler_params=pltpu.CompilerParams(dimension_semantics=("arbitrary",),
                                             vmem_limit_bytes=_vmem_limit(blocks)),
        name="ada",
    )(c, w, b)


def _ffn_kernel(x_ref, ng_ref, sh_ref, sc_ref, g_ref, wg_ref, wu_ref, wd_ref, o_ref, n_ref, acc_ref):
    j = pl.program_id(1)

    @pl.when(j == 0)
    def _():
        n = _rmsnorm_mod(x_ref[...], ng_ref[...], sh_ref[...], sc_ref[...])
        n_ref[...] = n.astype(BF16)
        acc_ref[...] = jnp.zeros_like(acc_ref)

    n = n_ref[...]
    a = _dot(n, wg_ref[...])
    u = _dot(n, wu_ref[...])
    hmid = (a * _sigmoid(a) * u).astype(BF16)
    acc_ref[...] += _dot(hmid, wd_ref[...])

    @pl.when(j == pl.num_programs(1) - 1)
    def _():
        o_ref[...] = x_ref[...] + 0.5 * g_ref[...] * acc_ref[...]


def _mod_spec(mod, tm, d):
    if mod.shape[0] == 1:
        return pl.BlockSpec((1, d), lambda i, j: (0, 0))
    return pl.BlockSpec((tm, d), lambda i, j: (i, 0))


def _ffn(x, norm_g, shift, scale, gate, wg, wu, wd, *, tm, tf):
    m, d = x.shape
    fp = wg.shape[1]
    tm = min(tm, m)
    assert m % tm == 0 and fp % tf == 0
    blocks = (2 * (2 * _nbytes((tm, d), F32) + 2 * _nbytes((d, tf), BF16) + _nbytes((tf, d), BF16))
              + _nbytes((tm, d), BF16) + _nbytes((tm, d), F32) + 2 * 3 * _nbytes((min(shift.shape[0], tm), d), F32))
    return pl.pallas_call(
        _ffn_kernel,
        out_shape=jax.ShapeDtypeStruct((m, d), F32),
        grid=(m // tm, fp // tf),
        in_specs=[pl.BlockSpec((tm, d), lambda i, j: (i, 0)),
                  pl.BlockSpec((1, d), lambda i, j: (0, 0)),
                  _mod_spec(shift, tm, d), _mod_spec(scale, tm, d), _mod_spec(gate, tm, d),
                  pl.BlockSpec((d, tf), lambda i, j: (0, j)),
                  pl.BlockSpec((d, tf), lambda i, j: (0, j)),
                  pl.BlockSpec((tf, d), lambda i, j: (j, 0))],
        out_specs=pl.BlockSpec((tm, d), lambda i, j: (i, 0)),
        scratch_shapes=[pltpu.VMEM((tm, d), BF16), pltpu.VMEM((tm, d), F32)],
        compiler_params=pltpu.CompilerParams(dimension_semantics=("parallel", "arbitrary"),
                                             vmem_limit_bytes=_vmem_limit(blocks)),
        name="ffn",
    )(x, norm_g, shift, scale, gate, wg, wu, wd)


def _subhead_rms_rope(z, gain, grp, cos_t, sin_a, sin_b):
    tn = z.shape[1]
    zz = (z * z).astype(BF16)
    ms = jnp.concatenate([_dot(zz[:, c:c + MXU_DIM], grp) for c in range(0, tn, MXU_DIM)], axis=1)
    y = z * lax.rsqrt(ms + EPS) * gain
    outs = []
    for c in range(0, tn, LANES):
        yc = y[:, c:c + LANES]
        outs.append(yc * cos_t + pltpu.roll(yc, LANES - ROT_DIM // 2, 1) * sin_a
                    + pltpu.roll(yc, ROT_DIM // 2, 1) * sin_b)
    return jnp.concatenate(outs, axis=1)


def _mix_in_kernel(h_ref, ng_ref, sh_ref, sc_ref, w_ref, qg_ref, kg_ref, grp_ref, cos_ref, sa_ref, sb_ref,
                   q_ref, k_ref, kb_ref, v_ref, vb_ref, glu_ref, ga_ref, gb_ref, u_ref, zc_ref, *, nq, transposed):
    j = pl.program_id(1)

    def attn_operand(x):
        return (x.T if transposed else x).astype(BF16)

    def store_v(z):
        if not transposed:
            vb_ref[...] = z.astype(BF16)
            return
        heads, _, tm = vb_ref.shape
        vb_ref[:, 0:V_DIM, :] = z.T.reshape(heads, V_DIM, tm).astype(BF16)
        vb_ref[:, V_DIM:, :] = jnp.ones((heads, V_ROWS - V_DIM, tm), BF16)

    @pl.when(j == 0)
    def _():
        u_ref[...] = _rmsnorm_mod(h_ref[...], ng_ref[...], sh_ref[...], sc_ref[...]).astype(BF16)

    z = _dot(u_ref[...], w_ref[...])

    @pl.when(j < nq)
    def _():
        r = _subhead_rms_rope(z, qg_ref[...], grp_ref[...], cos_ref[...], sa_ref[...], sb_ref[...])
        q_ref[...] = attn_operand(r * QK_SCALE_LOG2)

    @pl.when((j >= nq) & (j < 2 * nq))
    def _():
        r = _subhead_rms_rope(z, kg_ref[...], grp_ref[...], cos_ref[...], sa_ref[...], sb_ref[...])
        k_ref[...] = r
        kb_ref[...] = r.astype(BF16)

    @pl.when((j >= 2 * nq) & (j < 3 * nq))
    def _():
        v_ref[...] = z
        store_v(z)

    @pl.when(j == 3 * nq)
    def _():
        zc_ref[...] = z

    @pl.when(j == 3 * nq + 1)
    def _():
        glu_ref[...] = zc_ref[...] * _sigmoid(z)

    @pl.when((j >= 3 * nq + 2) & (j < 3 * nq + 4))
    def _():
        ga_ref[...] = _sigmoid(z)

    @pl.when(j >= 3 * nq + 4)
    def _():
        gb_ref[...] = _sigmoid(z)


def _mix_in(h, norm_g, shift, scale, w_in, q_gain, k_gain, grp, cos_t, sin_a, sin_b, *, tm, transposed):
    m, d = h.shape
    tn = d // 2
    nq = QK_COLS // tn
    assert QK_COLS % tn == 0 and tn % LANES == 0
    ncol = w_in.shape[1] // tn
    assert ncol == 3 * nq + 6
    tm = min(tm, m)
    assert m % tm == 0

    def col(lo):
        return lambda i, j: (i, jnp.clip(j - lo, 0, nq - 1))

    def col2(lo):
        return lambda i, j: (i, jnp.clip(j - lo, 0, 1))

    row = lambda i, j: (i, 0)
    const = lambda i, j: (0, 0)
    blocks = (2 * (_nbytes((tm, d), F32) + _nbytes((d, tn), BF16) + 3 * _nbytes((tm, LANES), F32)
                   + 6 * _nbytes((tm, tn), F32) + 3 * _nbytes((tm, tn), BF16))
              + _nbytes((tm, d), BF16) + _nbytes((tm, tn), F32) + 2 * 2 * _nbytes((min(shift.shape[0], tm), d), F32))
    gsz = grp.shape[0]
    if transposed:
        assert tm % LANES == 0 and tn % V_DIM == 0
        q_shape = jax.ShapeDtypeStruct((QK_COLS, m), BF16)
        q_spec = pl.BlockSpec((tn, tm), lambda i, j: (jnp.clip(j, 0, nq - 1), i))
        v_shape = jax.ShapeDtypeStruct((N_HEADS, V_ROWS, m), BF16)
        v_spec = pl.BlockSpec((tn // V_DIM, V_ROWS, tm), lambda i, j: (jnp.clip(j - 2 * nq, 0, nq - 1), 0, i))
    else:
        q_shape = jax.ShapeDtypeStruct((m, QK_COLS), BF16)
        q_spec = pl.BlockSpec((tm, tn), col(0))
        v_shape = jax.ShapeDtypeStruct((m, ATTN_WIDTH), BF16)
        v_spec = pl.BlockSpec((tm, tn), col(2 * nq))
    outs = pl.pallas_call(
        functools.partial(_mix_in_kernel, nq=nq, transposed=transposed),
        out_shape=[q_shape,
                   jax.ShapeDtypeStruct((m, QK_COLS), F32),
                   jax.ShapeDtypeStruct((m, QK_COLS), BF16),
                   jax.ShapeDtypeStruct((m, ATTN_WIDTH), F32),
                   v_shape,
                   jax.ShapeDtypeStruct((m, tn), F32),
                   jax.ShapeDtypeStruct((m, d), F32),
                   jax.ShapeDtypeStruct((m, d), F32)],
        grid=(m // tm, ncol),
        in_specs=[pl.BlockSpec((tm, d), row),
                  pl.BlockSpec((1, d), const),
                  _mod_spec(shift, tm, d), _mod_spec(scale, tm, d),
                  pl.BlockSpec((d, tn), lambda i, j: (0, j)),
                  pl.BlockSpec((1, tn), const), pl.BlockSpec((1, tn), const),
                  pl.BlockSpec((gsz, gsz), const),
                  pl.BlockSpec((tm, LANES), row), pl.BlockSpec((tm, LANES), row), pl.BlockSpec((tm, LANES), row)],
        out_specs=[q_spec,
                   pl.BlockSpec((tm, tn), col(nq)), pl.BlockSpec((tm, tn), col(nq)),
                   pl.BlockSpec((tm, tn), col(2 * nq)), v_spec,
                   pl.BlockSpec((tm, tn), row),
                   pl.BlockSpec((tm, tn), col2(3 * nq + 2)),
                   pl.BlockSpec((tm, tn), col2(3 * nq + 4))],
        scratch_shapes=[pltpu.VMEM((tm, d), BF16), pltpu.VMEM((tm, tn), F32)],
        compiler_params=pltpu.CompilerParams(dimension_semantics=("parallel", "arbitrary"),
                                             vmem_limit_bytes=_vmem_limit(blocks)),
        name="mix_in",
    )(h, norm_g, shift, scale, w_in, q_gain, k_gain, grp, cos_t, sin_a, sin_b)
    return outs


def _lambda_value(lam_ref, lam_init):
    lv = lam_ref[...]
    l1 = jnp.sum(lv[0:1] * lv[1:2], axis=-1, keepdims=True)
    l2 = jnp.sum(lv[2:3] * lv[3:4], axis=-1, keepdims=True)
    return jnp.exp(l1) - jnp.exp(l2) + lam_init


def _subln(o, gain, lam_init):
    ms = jnp.mean(o * o, axis=-1, keepdims=True)
    return o * lax.rsqrt(ms + EPS) * gain * (1.0 - lam_init)


def _attn_kernel(qi_ref, kj_ref, lam_ref, sg_ref, qt_ref, k_ref, vt_ref, o_ref,
                 qz_ref, m_ref, acc_ref, *, tq, tk, qc, lam_init):
    s_idx = pl.program_id(1)
    qi = qi_ref[s_idx]
    kj = kj_ref[s_idx]
    q_lo = qi * tq
    k_lo = kj * tk
    last_kj = (q_lo + tq - 1) // tk

    @pl.when(kj == 0)
    def _():
        qt = qt_ref[...]
        row = lax.broadcasted_iota(jnp.int32, qt.shape, 0)
        zero = jnp.zeros_like(qt)
        qz_ref[0] = jnp.where(row < HEAD_DIM, qt, zero)
        qz_ref[1] = jnp.where(row >= HEAD_DIM, qt, zero)
        m_ref[...] = jnp.full_like(m_ref, NEG_INF)
        acc_ref[...] = jnp.zeros_like(acc_ref)

    def step(delta):
        k = k_ref[...]
        vt = vt_ref[...]
        chunks = []
        for c0 in range(0, tq, qc):
            if delta is None or c0 >= delta + tk - 1:
                chunks.append((c0, False))
            elif c0 + qc - 1 >= delta:
                chunks.append((c0, True))

        def scores(c0):
            return [_dot(k, qz_ref[sub, :, c0:c0 + qc]) for sub in range(2)]

        nxt = scores(chunks[0][0])
        for idx, (c0, masked) in enumerate(chunks):
            cur = nxt
            if idx + 1 < len(chunks):
                nxt = scores(chunks[idx + 1][0])
            cols = slice(c0, c0 + qc)
            if masked:
                q_pos = c0 + lax.broadcasted_iota(jnp.int32, (tk, qc), 1)
                k_pos = delta + lax.broadcasted_iota(jnp.int32, (tk, qc), 0)
                keep = q_pos >= k_pos
            probs, alphas, m_news = [], [], []
            for sub in range(2):
                st = jnp.where(keep, cur[sub], NEG_INF) if masked else cur[sub]
                m_old = m_ref[sub, :, cols]
                m_new = jnp.maximum(m_old, jnp.max(st, axis=0, keepdims=True))
                alphas.append(jnp.exp2(m_old - m_new))
                probs.append(jnp.exp2(st - m_new).astype(BF16))
                m_news.append(m_new)
            for sub in range(2):
                acc_ref[sub, :, cols] = alphas[sub] * acc_ref[sub, :, cols] + _dot(vt, probs[sub])
                m_ref[sub, :, cols] = m_news[sub]

    rel = k_lo - q_lo

    @pl.when(rel < 0)
    def _():
        step(None)

    for delta in range(0, tq, tk):
        @pl.when(rel == delta)
        def _(delta=delta):
            step(delta)

    @pl.when(kj == last_kj)
    def _():
        lam = _lambda_value(lam_ref, lam_init)
        num = [acc_ref[sub, 0:V_DIM, :] for sub in range(2)]
        den = [acc_ref[sub, V_DIM:V_DIM + 1, :] for sub in range(2)]
        ot = num[0] * (1.0 / den[0]) - lam * (num[1] * (1.0 / den[1]))
        ms = jnp.mean(ot * ot, axis=0, keepdims=True)
        ot = ot * lax.rsqrt(ms + EPS) * sg_ref[...] * (1.0 - lam_init)
        o_ref[...] = ot.T.astype(o_ref.dtype)


def _prompt_attention(qt, k, vt, lam_vecs, subln_col, *, tq, tk, lam_init):
    t = k.shape[0]
    tq = min(tq, t)
    tk = min(tk, t)
    qc = min(MXU_DIM, tq)
    assert t % tq == 0 and tq % tk == 0 and tq % qc == 0
    steps = [(a, b) for a in range(t // tq) for b in range((a * tq + tq - 1) // tk + 1)]
    qi_arr = jnp.asarray([a for a, _ in steps], jnp.int32)
    kj_arr = jnp.asarray([b for _, b in steps], jnp.int32)
    blocks = (2 * (2 * _nbytes((tq, V_DIM), BF16) + 2 * _nbytes((tk, V_ROWS), BF16))
              + 2 * _nbytes((tq, V_DIM), BF16) + 2 * _nbytes((SUBLANES, tq), F32) + 2 * _nbytes((tq, V_ROWS), F32)
              + 16 * _nbytes((tk, qc), F32))
    grid_spec = pltpu.PrefetchScalarGridSpec(
        num_scalar_prefetch=2,
        grid=(N_HEADS, len(steps)),
        in_specs=[pl.BlockSpec((4, HEAD_DIM), lambda h, s, qi, kj: (0, 0)),
                  pl.BlockSpec((V_DIM, 1), lambda h, s, qi, kj: (0, 0)),
                  pl.BlockSpec((V_DIM, tq), lambda h, s, qi, kj: (h, qi[s])),
                  pl.BlockSpec((tk, V_DIM), lambda h, s, qi, kj: (kj[s], h)),
                  pl.BlockSpec((None, V_ROWS, tk), lambda h, s, qi, kj: (h, 0, kj[s]))],
        out_specs=pl.BlockSpec((tq, V_DIM), lambda h, s, qi, kj: (qi[s], h)),
        scratch_shapes=[pltpu.VMEM((2, V_DIM, tq), BF16),
                        pltpu.VMEM((2, 1, tq), F32),
                        pltpu.VMEM((2, V_ROWS, tq), F32)])
    return pl.pallas_call(
        functools.partial(_attn_kernel, tq=tq, tk=tk, qc=qc, lam_init=lam_init),
        out_shape=jax.ShapeDtypeStruct((t, ATTN_WIDTH), BF16),
        grid_spec=grid_spec,
        compiler_params=pltpu.CompilerParams(dimension_semantics=("parallel", "arbitrary"),
                                             vmem_limit_bytes=_vmem_limit(blocks)),
        name="prompt_attn",
    )(qi_arr, kj_arr, lam_vecs, subln_col, qt, k, vt)


def _reduce_positions(x, op):
    n = x.shape[0]
    if n > REDUCE_WAYS and n % REDUCE_WAYS == 0:
        x = op(x.reshape(n // REDUCE_WAYS, REDUCE_WAYS, *x.shape[1:]), axis=0)
    return op(x, axis=0)


def _decode_kernel(pt_ref, lam_ref, sg_ref, red_ref, q_ref, kn_ref, vn_ref, *rest, pages, lam_init):
    k_refs, v_refs = rest[:pages], rest[pages:2 * pages]
    o_ref, m_ref, l_ref, acc_ref = rest[2 * pages:]
    p_idx = pl.program_id(1)

    @pl.when(p_idx == 0)
    def _():
        m_ref[...] = jnp.full_like(m_ref, NEG_INF)
        l_ref[...] = jnp.zeros_like(l_ref)
        acc_ref[...] = jnp.zeros_like(acc_ref)

    q = q_ref[0]

    def page_scores(k_ref):
        prod = (k_ref[...] * q[None]).reshape(PAGE_SIZE * N_HEADS, V_DIM).astype(BF16)
        return _dot(prod, red_ref[...])

    state = [(m_ref[sub], l_ref[sub], acc_ref[sub]) for sub in range(2)]
    nxt = page_scores(k_refs[0])
    for i in range(pages):
        cur = nxt
        if i + 1 < pages:
            nxt = page_scores(k_refs[i + 1])
        v = v_refs[i][...]
        for sub in range(2):
            s = cur[:, sub * LANES:(sub + 1) * LANES].reshape(PAGE_SIZE, N_HEADS, LANES)
            m_old, l_old, acc_old = state[sub]
            m_new = jnp.maximum(m_old, _reduce_positions(s, jnp.max))
            alpha = jnp.exp2(m_old - m_new)
            p = jnp.exp2(s - m_new[None])
            state[sub] = (m_new, alpha * l_old + _reduce_positions(p, jnp.sum),
                          alpha * acc_old + _reduce_positions(p * v, jnp.sum))
    for sub in range(2):
        m_ref[sub], l_ref[sub], acc_ref[sub] = state[sub]

    @pl.when(p_idx == pl.num_programs(1) - 1)
    def _():
        kn = jnp.broadcast_to(kn_ref[...], (SUBLANES, N_HEADS, V_DIM))
        vn = jnp.broadcast_to(vn_ref[...], (SUBLANES, N_HEADS, V_DIM))
        prod = (kn * q[None]).reshape(SUBLANES * N_HEADS, V_DIM).astype(BF16)
        s_all = _dot(prod, red_ref[...])
        lam = _lambda_value(lam_ref, lam_init)
        outs = []
        for sub in range(2):
            s = s_all[:, sub * LANES:(sub + 1) * LANES].reshape(SUBLANES, N_HEADS, LANES)[0]
            m_old = m_ref[sub]
            m_new = jnp.maximum(m_old, s)
            alpha = jnp.exp2(m_old - m_new)
            p = jnp.exp2(s - m_new)
            l_fin = alpha * l_ref[sub] + p
            acc_fin = alpha * acc_ref[sub] + p * vn[0]
            outs.append(acc_fin / l_fin)
        o = outs[0] - lam * outs[1]
        o_ref[0] = _subln(o, sg_ref[...], lam_init)


def _decode_attention(page_table, q, k_new, v_new, cache_k, cache_v, lam_vecs, subln_g, red, *, lam_init):
    db, n_pages = page_table.shape
    pages = math.gcd(DECODE_PAGES_PER_STEP, n_pages)
    pt_flat = page_table.reshape(-1)
    blk = (1, N_HEADS, V_DIM)
    page_blk = (None, PAGE_SIZE, N_HEADS, V_DIM)
    page_bytes = _nbytes((PAGE_SIZE, N_HEADS, V_DIM), F32)
    blocks = 2 * 2 * pages * page_bytes + (2 * pages + 8) * page_bytes

    def page_spec(i):
        return pl.BlockSpec(page_blk, lambda b, p, pt: (pt[b * n_pages + p * pages + i], 0, 0, 0))

    grid_spec = pltpu.PrefetchScalarGridSpec(
        num_scalar_prefetch=1,
        grid=(db, n_pages // pages),
        in_specs=[pl.BlockSpec((4, HEAD_DIM), lambda b, p, pt: (0, 0)),
                  pl.BlockSpec((1, V_DIM), lambda b, p, pt: (0, 0)),
                  pl.BlockSpec((V_DIM, 2 * LANES), lambda b, p, pt: (0, 0)),
                  pl.BlockSpec(blk, lambda b, p, pt: (b, 0, 0)),
                  pl.BlockSpec(blk, lambda b, p, pt: (b, 0, 0)),
                  pl.BlockSpec(blk, lambda b, p, pt: (b, 0, 0))]
                 + [page_spec(i) for i in range(pages)] * 2,
        out_specs=pl.BlockSpec(blk, lambda b, p, pt: (b, 0, 0)),
        scratch_shapes=[pltpu.VMEM((2, N_HEADS, LANES), F32), pltpu.VMEM((2, N_HEADS, LANES), F32),
                        pltpu.VMEM((2, N_HEADS, V_DIM), F32)])
    return pl.pallas_call(
        functools.partial(_decode_kernel, pages=pages, lam_init=lam_init),
        out_shape=jax.ShapeDtypeStruct((db, N_HEADS, V_DIM), F32),
        grid_spec=grid_spec,
        compiler_params=pltpu.CompilerParams(dimension_semantics=("parallel", "arbitrary"),
                                             vmem_limit_bytes=_vmem_limit(blocks)),
        name="decode_attn",
    )(pt_flat, lam_vecs, subln_g, red, q, k_new, v_new, *([cache_k] * pages), *([cache_v] * pages))


def _mix_out_tail(y, cb_ref, lng_ref, lnb_ref, wco_ref, bco_ref, o_ref, wao_ref, ga_ref, gb_ref, h_ref, g_ref,
                  wout_ref, out_ref):
    y = y + cb_ref[...]
    mu = jnp.mean(y, axis=-1, keepdims=True)
    yc = y - mu
    var = jnp.mean(yc * yc, axis=-1, keepdims=True)
    cv = yc * lax.rsqrt(var + EPS) * lng_ref[...] + lnb_ref[...]
    cv = (cv * _sigmoid(cv)).astype(BF16)
    b_out = _dot(cv, wco_ref[...]) + bco_ref[...]
    a_out = _dot(o_ref[...].astype(BF16), wao_ref[...])
    mrg = (ga_ref[...] * a_out + gb_ref[...] * b_out).astype(BF16)
    out_ref[...] = h_ref[...] + g_ref[...] * _dot(mrg, wout_ref[...])


def _mix_out_prompt_kernel(glu_ref, halo_ref, cw_ref, cb_ref, lng_ref, lnb_ref, wco_ref, bco_ref, o_ref, wao_ref,
                           ga_ref, gb_ref, h_ref, g_ref, wout_ref, out_ref, ext_ref, y_ref, *, tm):
    i = pl.program_id(0)
    c = glu_ref.shape[1]

    @pl.when(i == 0)
    def _():
        ext_ref[0:CONV_HALO, :] = jnp.zeros((CONV_HALO, c), F32)

    @pl.when(i > 0)
    def _():
        ext_ref[0:CONV_HALO, :] = halo_ref[...]

    ext_ref[CONV_HALO:, :] = glu_ref[...]
    rc = min(CONV_ROW_CHUNK, tm)
    off = CONV_HALO - CONV_STATE
    for r0 in range(0, tm, rc):
        for c0 in range(0, c, LANES):
            acc = jnp.zeros((rc, LANES), F32)
            for j in range(CONV_WIDTH):
                acc = acc + cw_ref[j:j + 1, c0:c0 + LANES] * ext_ref[r0 + off + j:r0 + off + j + rc, c0:c0 + LANES]
            y_ref[r0:r0 + rc, c0:c0 + LANES] = acc
    _mix_out_tail(y_ref[...], cb_ref, lng_ref, lnb_ref, wco_ref, bco_ref, o_ref, wao_ref, ga_ref, gb_ref, h_ref,
                  g_ref, wout_ref, out_ref)


def _mix_out_sample_kernel(win_ref, cw_ref, cb_ref, lng_ref, lnb_ref, wco_ref, bco_ref, o_ref, wao_ref,
                           ga_ref, gb_ref, h_ref, g_ref, wout_ref, out_ref):
    y = cw_ref[0:1, :] * win_ref[0]
    for j in range(1, CONV_WIDTH):
        y = y + cw_ref[j:j + 1, :] * win_ref[j]
    _mix_out_tail(y, cb_ref, lng_ref, lnb_ref, wco_ref, bco_ref, o_ref, wao_ref, ga_ref, gb_ref, h_ref, g_ref,
                  wout_ref, out_ref)


def _resident(shape):
    return pl.BlockSpec(shape, lambda i: (0,) * len(shape), pipeline_mode=pl.Buffered(1))


def _mix_out_common_specs(tm, c, d, aw, gate):
    row = lambda i: (i, 0)
    g_spec = (pl.BlockSpec((1, d), lambda i: (0, 0)) if gate.shape[0] == 1 else pl.BlockSpec((tm, d), row))
    return [_resident((CONV_HALO, c)), _resident((1, c)), _resident((1, c)), _resident((1, c)),
            _resident((c, d)), _resident((1, d)),
            pl.BlockSpec((tm, aw), row), _resident((aw, d)),
            pl.BlockSpec((tm, d), row), pl.BlockSpec((tm, d), row), pl.BlockSpec((tm, d), row), g_spec,
            _resident((d, d))]


def _mix_out_bytes(tm, c, d, aw, o_dtype):
    return (2 * (_nbytes((tm, aw), o_dtype) + 4 * _nbytes((tm, d), F32) + _nbytes((tm, d), F32))
            + _nbytes((c, d), BF16) + _nbytes((aw, d), BF16) + _nbytes((d, d), BF16)
            + 8 * _nbytes((tm, d), F32))


def _mix_out_prompt(glu, conv_w, conv_b, ln_g, ln_b, w_co, b_co, o, w_ao, ga, gb, h, gate, w_out, *, tm):
    m, c = glu.shape
    d = h.shape[1]
    aw = o.shape[1]
    tm = min(tm, m)
    assert m % tm == 0 and tm % CONV_HALO == 0
    hb = tm // CONV_HALO
    blocks = _mix_out_bytes(tm, c, d, aw, o.dtype) + 4 * _nbytes((tm + CONV_HALO, c), F32)
    return pl.pallas_call(
        functools.partial(_mix_out_prompt_kernel, tm=tm),
        out_shape=jax.ShapeDtypeStruct((m, d), F32),
        grid=(m // tm,),
        in_specs=[pl.BlockSpec((tm, c), lambda i: (i, 0)),
                  pl.BlockSpec((CONV_HALO, c), lambda i: (jnp.maximum(i * hb - 1, 0), 0))]
                 + _mix_out_common_specs(tm, c, d, aw, gate),
        out_specs=pl.BlockSpec((tm, d), lambda i: (i, 0)),
        scratch_shapes=[pltpu.VMEM((tm + CONV_HALO, c), F32), pltpu.VMEM((tm, c), F32)],
        compiler_params=pltpu.CompilerParams(dimension_semantics=("arbitrary",),
                                             vmem_limit_bytes=_vmem_limit(blocks)),
        name="mix_out_prompt",
    )(glu, glu, conv_w, conv_b, ln_g, ln_b, w_co, b_co, o, w_ao, ga, gb, h, gate, w_out)


def _mix_out_sample(win, conv_w, conv_b, ln_g, ln_b, w_co, b_co, o, w_ao, ga, gb, h, gate, w_out):
    _, m, c = win.shape
    d = h.shape[1]
    aw = o.shape[1]
    blocks = _mix_out_bytes(m, c, d, aw, o.dtype) + 2 * _nbytes(win.shape, F32)
    return pl.pallas_call(
        _mix_out_sample_kernel,
        out_shape=jax.ShapeDtypeStruct((m, d), F32),
        grid=(1,),
        in_specs=[pl.BlockSpec(win.shape, lambda i: (0, 0, 0))] + _mix_out_common_specs(m, c, d, aw, gate),
        out_specs=pl.BlockSpec((m, d), lambda i: (0, 0)),
        compiler_params=pltpu.CompilerParams(dimension_semantics=("arbitrary",),
                                             vmem_limit_bytes=_vmem_limit(blocks)),
        name="mix_out_sample",
    )(win, conv_w, conv_b, ln_g, ln_b, w_co, b_co, o, w_ao, ga, gb, h, gate, w_out)


def _rope_tables(pos):
    half = ROT_DIM // 2
    inv = jnp.power(ROPE_THETA, -jnp.arange(half, dtype=F32) * 2.0 / ROT_DIM)
    ang = pos.astype(F32)[:, None] * inv[None, :]
    cos, sin = jnp.cos(ang), jnp.sin(ang)
    t = pos.shape[0]
    pad = jnp.zeros((t, HEAD_DIM - ROT_DIM), F32)
    zero = jnp.zeros((t, half), F32)
    cos_t = jnp.concatenate([cos, cos, pad + 1.0], axis=1)
    sin_a = jnp.concatenate([-sin, zero, pad], axis=1)
    sin_b = jnp.concatenate([zero, sin, pad], axis=1)
    rep = LANES // HEAD_DIM
    return jnp.tile(cos_t, (1, rep)), jnp.tile(sin_a, (1, rep)), jnp.tile(sin_b, (1, rep))


def _group_mean_matrix(n):
    g = jnp.arange(n) // HEAD_DIM
    return jnp.where(g[:, None] == g[None, :], 1.0 / HEAD_DIM, 0.0).astype(BF16)


def _half_sum_matrix():
    lane_half = jnp.arange(V_DIM) // HEAD_DIM
    col_half = jnp.arange(2 * LANES) // LANES
    return (lane_half[:, None] == col_half[None, :]).astype(BF16)


def _pad_cols(w, mult):
    return jnp.pad(w, ((0, 0), (0, -w.shape[1] % mult)))


def _pad_rows(w, mult):
    return jnp.pad(w, ((0, -w.shape[0] % mult), (0, 0)))


def _tile_config(seq, d_ff):
    return dict(ffn_tm=min(512, seq), ffn_tf=min(512, pl.cdiv(d_ff, LANES) * LANES), mix_in_tm=min(512, seq),
                attn_tq=min(1024, seq), attn_tk=min(512, seq), mix_out_tm=min(256, seq))


def kernel(x_prompt, x_sample, cache_k, cache_v, state_conv, page_table, c_prompt, c_sample, w_ada, b_ada, norm1, w1_gate, w1_up, w1_down, norm2, w_in, q_norm, k_norm, lam_q1, lam_k1, lam_q2, lam_k2, subln, w_attn_out, conv_w, conv_b, conv_ln_g, conv_ln_b, w_conv_out, b_conv_out, w_out, norm3, w2_gate, w2_up, w2_down):
    bsz, seq, d = x_prompt.shape
    db, ts, _ = x_sample.shape
    depth = w_ada.shape[0]
    assert bsz == 1 and ts == 1 and depth == 1, "kernel is specialised to one prompt sequence, one new token, one layer"
    c = conv_w.shape[-1]
    d_ff = w1_gate.shape[-1]
    past_len = page_table.shape[1] * PAGE_SIZE
    cfg = _tile_config(seq, d_ff)
    lam_init = _lambda_init(0)

    tf = cfg["ffn_tf"]
    w1g = _pad_cols(w1_gate[0].astype(BF16), tf)
    w1u = _pad_cols(w1_up[0].astype(BF16), tf)
    w1d = _pad_rows(w1_down[0].astype(BF16), tf)
    w2g = _pad_cols(w2_gate[0].astype(BF16), tf)
    w2u = _pad_cols(w2_up[0].astype(BF16), tf)
    w2d = _pad_rows(w2_down[0].astype(BF16), tf)
    w_in_b = w_in[0].astype(BF16)
    w_ao = w_attn_out[0].astype(BF16)
    w_co = w_conv_out[0].astype(BF16)
    w_o = w_out[0].astype(BF16)

    n_c = bsz + db
    c_all = jnp.concatenate([c_prompt, c_sample, jnp.zeros((-n_c % SUBLANES, d), F32)], axis=0)
    ada = _ada(c_all, w_ada[0], b_ada)
    ada_p = [ada[0:1, i * d:(i + 1) * d] for i in range(N_MOD)]
    ada_s = [ada[1:1 + db, i * d:(i + 1) * d] for i in range(N_MOD)]

    qg = jnp.tile(q_norm[0], c // HEAD_DIM)[None]
    kg = jnp.tile(k_norm[0], c // HEAD_DIM)[None]
    grp = _group_mean_matrix(min(MXU_DIM, c))
    red = _half_sum_matrix()
    lam_vecs = jnp.stack([lam_q1[0], lam_k1[0], lam_q2[0], lam_k2[0]])
    cw = jnp.pad(conv_w[0], ((0, CONV_HALO - CONV_WIDTH), (0, 0)))

    def layer_front(x, mods, pos, tm_ffn, tm_mix, transposed):
        sh1, sc1, g1, sh2, sc2 = mods[:5]
        h = _ffn(x, norm1, sh1, sc1, g1, w1g, w1u, w1d, tm=tm_ffn, tf=tf)
        cos_t, sin_a, sin_b = _rope_tables(pos)
        return (h,) + tuple(_mix_in(h, norm2, sh2, sc2, w_in_b, qg, kg, grp, cos_t, sin_a, sin_b, tm=tm_mix,
                                    transposed=transposed))

    def layer_back(h, mods, tm_ffn):
        sh3, sc3, g3 = mods[6:9]
        return _ffn(h, norm3, sh3, sc3, g3, w2g, w2u, w2d, tm=tm_ffn, tf=tf)

    xp = x_prompt[0]
    pos_p = jnp.arange(seq, dtype=jnp.int32)
    hp, qtp, kp, kpb, vp, vtp, glu_p, ga_p, gb_p = layer_front(xp, ada_p, pos_p, cfg["ffn_tm"], cfg["mix_in_tm"],
                                                                 transposed=True)
    op = _prompt_attention(qtp, kpb, vtp, lam_vecs, subln.reshape(V_DIM, 1), tq=cfg["attn_tq"], tk=cfg["attn_tk"],
                           lam_init=lam_init)
    hp2 = _mix_out_prompt(glu_p, cw, conv_b, conv_ln_g, conv_ln_b, w_co, b_conv_out, op, w_ao, ga_p, gb_p, hp,
                          ada_p[5], w_o, tm=cfg["mix_out_tm"])
    yp = layer_back(hp2, ada_p, cfg["ffn_tm"])

    xs = x_sample[:, 0]
    pos_s = jnp.full((db,), past_len, jnp.int32)
    hs, qs, ks, _, vs, _, glu_s, ga_s, gb_s = layer_front(xs, ada_s, pos_s, db, db, transposed=False)
    os_ = _decode_attention(page_table, qs.astype(F32).reshape(db, N_HEADS, V_DIM), ks.reshape(db, N_HEADS, V_DIM),
                            vs.reshape(db, N_HEADS, V_DIM), cache_k.reshape(cache_k.shape[1:]),
                            cache_v.reshape(cache_v.shape[1:]), lam_vecs, subln, red,
                            lam_init=lam_init)
    padded_s = jnp.concatenate([state_conv[0], glu_s[:, None, :]], axis=1)
    win = jnp.pad(padded_s.transpose(1, 0, 2), ((0, CONV_HALO - CONV_WIDTH), (0, 0), (0, 0)))
    hs2 = _mix_out_sample(win, cw, conv_b, conv_ln_g, conv_ln_b, w_co, b_conv_out,
                          os_.reshape(db, ATTN_WIDTH), w_ao, ga_s, gb_s, hs, ada_s[5], w_o)
    ys = layer_back(hs2, ada_s, db)

    return (yp[None],
            ys[:, None, :],
            kp.reshape(1, 1, seq, N_HEADS, 2 * HEAD_DIM),
            vp.reshape(1, 1, seq, N_HEADS, V_DIM),
            glu_p[seq - CONV_STATE:][None, None],
            ks.reshape(1, db, 1, N_HEADS, 2 * HEAD_DIM),
            vs.reshape(1, db, 1, N_HEADS, V_DIM),
            padded_s[None, :, 1:, :])
```

```python
import functools
import math

import jax
import jax.numpy as jnp
from jax import lax
from jax.experimental import pallas as pl
from jax.experimental.pallas import tpu as pltpu

N_HEADS = 8
HEAD_DIM = 64
V_DIM = 2 * HEAD_DIM
QK_COLS = N_HEADS * 2 * HEAD_DIM
ATTN_WIDTH = N_HEADS * V_DIM
ROT_DIM = HEAD_DIM // 4
ROPE_THETA = 500000.0
PAGE_SIZE = 128
CONV_WIDTH = 31
CONV_STATE = CONV_WIDTH - 1
EPS = 1e-6
NEG_INF = -1e30
N_MOD = 9
QK_SCALE_LOG2 = HEAD_DIM ** -0.5 * math.log2(math.e)
V_ROWS = V_DIM + 16

LANES = 128
SUBLANES = 8
MXU_DIM = 256
VMEM_BYTES_V7X = 64 * 1024 * 1024
VMEM_LIMIT_CAP = VMEM_BYTES_V7X - 6 * 1024 * 1024

CONV_HALO = 32
CONV_ROW_CHUNK = 128
MIX_IN_ROW_CHUNK = 256
FFN_ROW_CHUNK = 256
DECODE_PAGES_FIRST = 8
DECODE_STATE_ROWS = 6
REDUCE_WAYS = 8

F32 = jnp.float32
BF16 = jnp.bfloat16


def _lambda_init(layer_idx):
    return 0.8 - 0.6 * math.exp(-0.3 * layer_idx)


def _vmem_limit(block_bytes):
    return int(min(VMEM_LIMIT_CAP, block_bytes + block_bytes // 4 + (4 << 20)))


def _nbytes(shape, dtype):
    return math.prod(shape) * jnp.dtype(dtype).itemsize


def _dot(a, b):
    return jnp.dot(a, b, preferred_element_type=F32)


def _sigmoid(x):
    return 1.0 / (1.0 + jnp.exp(-x))


def _rmsnorm_mod(x, gain, shift, scale):
    ms = jnp.mean(x * x, axis=-1, keepdims=True)
    n = x * lax.rsqrt(ms + EPS) * gain
    return n * (1.0 + scale) + shift


def _ada_kernel(c_ref, w_ref, b_ref, o_ref):
    c = c_ref[...]
    a = (c * _sigmoid(c)).astype(BF16)
    o_ref[...] = _dot(a, w_ref[...].astype(BF16)) + b_ref[...]


def _ada(c, w, b):
    m, d = c.shape
    n = w.shape[1]
    tn = d // 2
    assert n % tn == 0 and tn % LANES == 0
    blocks = 2 * (_nbytes((m, d), F32) + _nbytes((d, tn), F32) + _nbytes((m, tn), F32))
    return pl.pallas_call(
        _ada_kernel,
        out_shape=jax.ShapeDtypeStruct((m, n), F32),
        grid=(n // tn,),
        in_specs=[pl.BlockSpec((m, d), lambda j: (0, 0)),
                  pl.BlockSpec((d, tn), lambda j: (0, j)),
                  pl.BlockSpec((1, tn), lambda j: (0, j))],
        out_specs=pl.BlockSpec((m, tn), lambda j: (0, j)),
        compiler_params=pltpu.CompilerParams(dimension_semantics=("arbitrary",),
                                             vmem_limit_bytes=_vmem_limit(blocks)),
        name="ada",
    )(c, w, b)


def _ffn_kernel(*refs, decode):
    if decode is None:
        x_ref, ng_ref, sh_ref, sc_ref, g_ref, wg_ref, wu_ref, wd_ref, o_ref, n_ref, acc_ref = refs
    else:
        pages = decode["pages"]
        (_, x_ref, ng_ref, sh_ref, sc_ref, g_ref, wg_ref, wu_ref, wd_ref, q_ref, red_ref), rest = refs[:11], refs[11:]
        k_refs, v_refs = rest[:pages], rest[pages:2 * pages]
        o_ref, st_out_ref, n_ref, acc_ref, st_ref = rest[2 * pages:]
    j = pl.program_id(1)

    @pl.when(j == 0)
    def _():
        n = _rmsnorm_mod(x_ref[...], ng_ref[...], sh_ref[...], sc_ref[...])
        n_ref[...] = n.astype(BF16)
        acc_ref[...] = jnp.zeros_like(acc_ref)

    tm, tf = n_ref.shape[0], wg_ref.shape[1]
    rc, cc = min(FFN_ROW_CHUNK, tm), min(MXU_DIM, tf)

    def ffn_piece(r0, c0):
        rows, cols = slice(r0, r0 + rc), slice(c0, c0 + cc)
        n = n_ref[rows, :]
        a = _dot(n, wg_ref[:, cols])
        u = _dot(n, wu_ref[:, cols])
        hmid = (a * _sigmoid(a) * u).astype(BF16)
        acc_ref[rows, :] += _dot(hmid, wd_ref[cols, :])

    pieces = [(r0, c0) for c0 in range(0, tf, cc) for r0 in range(0, tm, rc)]
    if decode is None:
        for piece in pieces:
            ffn_piece(*piece)
    else:
        step = pl.program_id(0) * pl.num_programs(1) + j
        valid = step < decode["n_units"]
        first = jnp.minimum(step, decode["n_units"] - 1) % decode["units_per_row"] == 0
        old = [st_ref[r] for r in range(DECODE_STATE_ROWS)]
        fresh = [jnp.full_like(old[0], NEG_INF)] * 2 + [jnp.zeros_like(old[0])] * 4
        state = [jnp.where(first, f, o) for f, o in zip(fresh, old)]
        per_piece = -(-pages // len(pieces))
        page = 0
        for piece in pieces:
            ffn_piece(*piece)
            for _ in range(per_piece):
                if page < pages:
                    state = _absorb_page(q_ref[0], red_ref, k_refs[page], v_refs[page], state)
                    page += 1
        for r in range(DECODE_STATE_ROWS):
            kept = jnp.where(valid, state[r], old[r])
            st_ref[r] = kept
            st_out_ref[0, r] = kept

    @pl.when(j == pl.num_programs(1) - 1)
    def _():
        o_ref[...] = x_ref[...] + 0.5 * g_ref[...] * acc_ref[...]


def _mod_spec(mod, tm, d):
    if mod.shape[0] == 1:
        return pl.BlockSpec((1, d), lambda i, j, *_: (0, 0))
    return pl.BlockSpec((tm, d), lambda i, j, *_: (i, 0))


def _ffn(x, norm_g, shift, scale, gate, wg, wu, wd, *, tm, tf, decode=None):
    m, d = x.shape
    fp = wg.shape[1]
    tm = min(tm, m)
    assert m % tm == 0 and fp % tf == 0
    nj = fp // tf
    blocks = (2 * (2 * _nbytes((tm, d), F32) + 2 * _nbytes((d, tf), BF16) + _nbytes((tf, d), BF16))
              + _nbytes((tm, d), BF16) + _nbytes((tm, d), F32) + 2 * 3 * _nbytes((min(shift.shape[0], tm), d), F32))
    in_specs = [pl.BlockSpec((tm, d), lambda i, j, *_: (i, 0)),
                pl.BlockSpec((1, d), lambda i, j, *_: (0, 0)),
                _mod_spec(shift, tm, d), _mod_spec(scale, tm, d), _mod_spec(gate, tm, d),
                pl.BlockSpec((d, tf), lambda i, j, *_: (0, j)),
                pl.BlockSpec((d, tf), lambda i, j, *_: (0, j)),
                pl.BlockSpec((tf, d), lambda i, j, *_: (j, 0))]
    out_spec = pl.BlockSpec((tm, d), lambda i, j, *_: (i, 0))
    scratch = [pltpu.VMEM((tm, d), BF16), pltpu.VMEM((tm, d), F32)]
    params = dict(dimension_semantics=("parallel" if decode is None else "arbitrary", "arbitrary"))
    if decode is None:
        return pl.pallas_call(
            functools.partial(_ffn_kernel, decode=None),
            out_shape=jax.ShapeDtypeStruct((m, d), F32),
            grid=(m // tm, nj), in_specs=in_specs, out_specs=out_spec, scratch_shapes=scratch,
            compiler_params=pltpu.CompilerParams(vmem_limit_bytes=_vmem_limit(blocks), **params),
            name="ffn",
        )(x, norm_g, shift, scale, gate, wg, wu, wd)

    pages, upr, row0, n_rows = decode["pages"], decode["units_per_row"], decode["row0"], decode["n_rows"]
    n_pages = pages * upr
    n_units = n_rows * upr
    assert n_units <= (m // tm) * nj
    page_bytes = _nbytes((PAGE_SIZE, N_HEADS, V_DIM), F32)
    blocks += 2 * 2 * pages * page_bytes + 8 * page_bytes

    def unit(i, j):
        return jnp.minimum(i * nj + j, n_units - 1)

    def page_spec(r):
        return pl.BlockSpec((None, PAGE_SIZE, N_HEADS, V_DIM),
                            lambda i, j, pt: (pt[(row0 + unit(i, j) // upr) * n_pages + (unit(i, j) % upr) * pages + r],
                                              0, 0, 0))

    state_blk = (1, DECODE_STATE_ROWS, N_HEADS, LANES)
    grid_spec = pltpu.PrefetchScalarGridSpec(
        num_scalar_prefetch=1,
        grid=(m // tm, nj),
        in_specs=in_specs + [pl.BlockSpec((1, N_HEADS, V_DIM), lambda i, j, pt: (row0 + unit(i, j) // upr, 0, 0)),
                             pl.BlockSpec((V_DIM, 2 * LANES), lambda i, j, pt: (0, 0))]
                 + [page_spec(r) for r in range(pages)] * 2,
        out_specs=[out_spec, pl.BlockSpec(state_blk, lambda i, j, pt: (unit(i, j) // upr, 0, 0, 0))],
        scratch_shapes=scratch + [pltpu.VMEM(state_blk[1:], F32)])
    kernel_decode = dict(pages=pages, units_per_row=upr, n_units=n_units)
    return pl.pallas_call(
        functools.partial(_ffn_kernel, decode=kernel_decode),
        out_shape=[jax.ShapeDtypeStruct((m, d), F32),
                   jax.ShapeDtypeStruct((n_rows,) + state_blk[1:], F32)],
        grid_spec=grid_spec,
        compiler_params=pltpu.CompilerParams(vmem_limit_bytes=_vmem_limit(blocks), **params),
        name="ffn_decode",
    )(decode["page_table"].reshape(-1), x, norm_g, shift, scale, gate, wg, wu, wd, decode["q"], decode["red"],
      *([decode["cache_k"]] * pages), *([decode["cache_v"]] * pages))


def _subhead_rms_rope(z, gain, grp, cos_t, sin_a, sin_b):
    tn = z.shape[1]
    zz = (z * z).astype(BF16)
    ms = jnp.concatenate([_dot(zz[:, c:c + MXU_DIM], grp) for c in range(0, tn, MXU_DIM)], axis=1)
    y = z * lax.rsqrt(ms + EPS) * gain
    outs = []
    for c in range(0, tn, LANES):
        yc = y[:, c:c + LANES]
        outs.append(yc * cos_t + pltpu.roll(yc, LANES - ROT_DIM // 2, 1) * sin_a
                    + pltpu.roll(yc, ROT_DIM // 2, 1) * sin_b)
    return jnp.concatenate(outs, axis=1)


def _mix_in_kernel(h_ref, ng_ref, sh_ref, sc_ref, w_ref, qg_ref, kg_ref, grp_ref, cos_ref, sa_ref, sb_ref,
                   q_ref, k_ref, kb_ref, v_ref, vb_ref, glu_ref, ga_ref, gb_ref, u_ref, zc_ref, *, nq, transposed):
    j = pl.program_id(1)

    tm = u_ref.shape[0]
    rc = min(MIX_IN_ROW_CHUNK, tm)

    def column_group(lo, hi, epilogue):
        @pl.when((j >= lo) & (j < hi))
        def _():
            for r0 in range(0, tm, rc):
                rows = slice(r0, r0 + rc)
                epilogue(rows, _dot(u_ref[rows, :], w_ref[...]))

    def rope(z, rows, gain_ref):
        return _subhead_rms_rope(z, gain_ref[...], grp_ref[...], cos_ref[rows, :], sa_ref[rows, :], sb_ref[rows, :])

    def q_epilogue(rows, z):
        r = rope(z, rows, qg_ref) * QK_SCALE_LOG2
        if transposed:
            q_ref[:, rows] = r.T.astype(BF16)
        else:
            q_ref[rows, :] = r.astype(BF16)

    def k_epilogue(rows, z):
        r = rope(z, rows, kg_ref)
        k_ref[rows, :] = r
        kb_ref[rows, :] = r.astype(BF16)

    def v_epilogue(rows, z):
        v_ref[rows, :] = z
        if transposed:
            heads = vb_ref.shape[0]
            vb_ref[:, 0:V_DIM, rows] = z.T.reshape(heads, V_DIM, rc).astype(BF16)
            vb_ref[:, V_DIM:, rows] = jnp.ones((heads, V_ROWS - V_DIM, rc), BF16)
        else:
            vb_ref[rows, :] = z.astype(BF16)

    def zc_epilogue(rows, z):
        zc_ref[rows, :] = z

    def glu_epilogue(rows, z):
        glu_ref[rows, :] = zc_ref[rows, :] * _sigmoid(z)

    def ga_epilogue(rows, z):
        ga_ref[rows, :] = _sigmoid(z)

    def gb_epilogue(rows, z):
        gb_ref[rows, :] = _sigmoid(z)

    @pl.when(j == 0)
    def _():
        u_ref[...] = _rmsnorm_mod(h_ref[...], ng_ref[...], sh_ref[...], sc_ref[...]).astype(BF16)

    column_group(0, nq, q_epilogue)
    column_group(nq, 2 * nq, k_epilogue)
    column_group(2 * nq, 3 * nq, v_epilogue)
    column_group(3 * nq, 3 * nq + 1, zc_epilogue)
    column_group(3 * nq + 1, 3 * nq + 2, glu_epilogue)
    column_group(3 * nq + 2, 3 * nq + 4, ga_epilogue)
    column_group(3 * nq + 4, 3 * nq + 6, gb_epilogue)


def _mix_in(h, norm_g, shift, scale, w_in, q_gain, k_gain, grp, cos_t, sin_a, sin_b, *, tm, transposed):
    m, d = h.shape
    tn = d // 2
    nq = QK_COLS // tn
    assert QK_COLS % tn == 0 and tn % LANES == 0
    ncol = w_in.shape[1] // tn
    assert ncol == 3 * nq + 6
    tm = min(tm, m)
    assert m % tm == 0

    def col(lo):
        return lambda i, j: (i, jnp.clip(j - lo, 0, nq - 1))

    def col2(lo):
        return lambda i, j: (i, jnp.clip(j - lo, 0, 1))

    row = lambda i, j: (i, 0)
    const = lambda i, j: (0, 0)
    blocks = (2 * (_nbytes((tm, d), F32) + _nbytes((d, tn), BF16) + 3 * _nbytes((tm, LANES), F32)
                   + 6 * _nbytes((tm, tn), F32) + 3 * _nbytes((tm, tn), BF16))
              + _nbytes((tm, d), BF16) + _nbytes((tm, tn), F32) + 2 * 2 * _nbytes((min(shift.shape[0], tm), d), F32))
    gsz = grp.shape[0]
    if transposed:
        assert tm % LANES == 0 and tn % V_DIM == 0
        q_shape = jax.ShapeDtypeStruct((QK_COLS, m), BF16)
        q_spec = pl.BlockSpec((tn, tm), lambda i, j: (jnp.clip(j, 0, nq - 1), i))
        v_shape = jax.ShapeDtypeStruct((N_HEADS, V_ROWS, m), BF16)
        v_spec = pl.BlockSpec((tn // V_DIM, V_ROWS, tm), lambda i, j: (jnp.clip(j - 2 * nq, 0, nq - 1), 0, i))
    else:
        q_shape = jax.ShapeDtypeStruct((m, QK_COLS), BF16)
        q_spec = pl.BlockSpec((tm, tn), col(0))
        v_shape = jax.ShapeDtypeStruct((m, ATTN_WIDTH), BF16)
        v_spec = pl.BlockSpec((tm, tn), col(2 * nq))
    outs = pl.pallas_call(
        functools.partial(_mix_in_kernel, nq=nq, transposed=transposed),
        out_shape=[q_shape,
                   jax.ShapeDtypeStruct((m, QK_COLS), F32),
                   jax.ShapeDtypeStruct((m, QK_COLS), BF16),
                   jax.ShapeDtypeStruct((m, ATTN_WIDTH), F32),
                   v_shape,
                   jax.ShapeDtypeStruct((m, tn), F32),
                   jax.ShapeDtypeStruct((m, d), F32),
                   jax.ShapeDtypeStruct((m, d), F32)],
        grid=(m // tm, ncol),
        in_specs=[pl.BlockSpec((tm, d), row),
                  pl.BlockSpec((1, d), const),
                  _mod_spec(shift, tm, d), _mod_spec(scale, tm, d),
                  pl.BlockSpec((d, tn), lambda i, j: (0, j)),
                  pl.BlockSpec((1, tn), const), pl.BlockSpec((1, tn), const),
                  pl.BlockSpec((gsz, gsz), const),
                  pl.BlockSpec((tm, LANES), row), pl.BlockSpec((tm, LANES), row), pl.BlockSpec((tm, LANES), row)],
        out_specs=[q_spec,
                   pl.BlockSpec((tm, tn), col(nq)), pl.BlockSpec((tm, tn), col(nq)),
                   pl.BlockSpec((tm, tn), col(2 * nq)), v_spec,
                   pl.BlockSpec((tm, tn), row),
                   pl.BlockSpec((tm, tn), col2(3 * nq + 2)),
                   pl.BlockSpec((tm, tn), col2(3 * nq + 4))],
        scratch_shapes=[pltpu.VMEM((tm, d), BF16), pltpu.VMEM((tm, tn), F32)],
        compiler_params=pltpu.CompilerParams(dimension_semantics=("parallel", "arbitrary"),
                                             vmem_limit_bytes=_vmem_limit(blocks)),
        name="mix_in",
    )(h, norm_g, shift, scale, w_in, q_gain, k_gain, grp, cos_t, sin_a, sin_b)
    return outs


def _lambda_value(lam_ref, lam_init):
    lv = lam_ref[...]
    l1 = jnp.sum(lv[0:1] * lv[1:2], axis=-1, keepdims=True)
    l2 = jnp.sum(lv[2:3] * lv[3:4], axis=-1, keepdims=True)
    return jnp.exp(l1) - jnp.exp(l2) + lam_init


def _subln(o, gain, lam_init):
    ms = jnp.mean(o * o, axis=-1, keepdims=True)
    return o * lax.rsqrt(ms + EPS) * gain * (1.0 - lam_init)


def _attn_kernel(qi_ref, kj_ref, lam_ref, sg_ref, qt_ref, k_ref, vt_ref, o_ref,
                 qz_ref, m_ref, acc_ref, *, tq, tk, qc, lam_init):
    s_idx = pl.program_id(1)
    qi = qi_ref[s_idx]
    kj = kj_ref[s_idx]
    q_lo = qi * tq
    k_lo = kj * tk
    last_kj = (q_lo + tq - 1) // tk

    @pl.when(kj == 0)
    def _():
        qt = qt_ref[...]
        row = lax.broadcasted_iota(jnp.int32, qt.shape, 0)
        zero = jnp.zeros_like(qt)
        qz_ref[0] = jnp.where(row < HEAD_DIM, qt, zero)
        qz_ref[1] = jnp.where(row >= HEAD_DIM, qt, zero)
        m_ref[...] = jnp.full_like(m_ref, NEG_INF)
        acc_ref[...] = jnp.zeros_like(acc_ref)

    def step(delta):
        k = k_ref[...]
        vt = vt_ref[...]
        chunks = []
        for c0 in range(0, tq, qc):
            if delta is None or c0 >= delta + tk - 1:
                chunks.append((c0, False))
            elif c0 + qc - 1 >= delta:
                chunks.append((c0, True))

        def scores(c0):
            return [_dot(k, qz_ref[sub, :, c0:c0 + qc]) for sub in range(2)]

        nxt = scores(chunks[0][0])
        for idx, (c0, masked) in enumerate(chunks):
            cur = nxt
            if idx + 1 < len(chunks):
                nxt = scores(chunks[idx + 1][0])
            cols = slice(c0, c0 + qc)
            if masked:
                q_pos = c0 + lax.broadcasted_iota(jnp.int32, (tk, qc), 1)
                k_pos = delta + lax.broadcasted_iota(jnp.int32, (tk, qc), 0)
                keep = q_pos >= k_pos
            probs, alphas, m_news = [], [], []
            for sub in range(2):
                st = jnp.where(keep, cur[sub], NEG_INF) if masked else cur[sub]
                m_old = m_ref[sub, :, cols]
                m_new = jnp.maximum(m_old, jnp.max(st, axis=0, keepdims=True))
                alphas.append(jnp.exp2(m_old - m_new))
                probs.append(jnp.exp2(st - m_new).astype(BF16))
                m_news.append(m_new)
            for sub in range(2):
                acc_ref[sub, :, cols] = alphas[sub] * acc_ref[sub, :, cols] + _dot(vt, probs[sub])
                m_ref[sub, :, cols] = m_news[sub]

    rel = k_lo - q_lo

    @pl.when(rel < 0)
    def _():
        step(None)

    for delta in range(0, tq, tk):
        @pl.when(rel == delta)
        def _(delta=delta):
            step(delta)

    @pl.when(kj == last_kj)
    def _():
        lam = _lambda_value(lam_ref, lam_init)
        num = [acc_ref[sub, 0:V_DIM, :] for sub in range(2)]
        den = [acc_ref[sub, V_DIM:V_DIM + 1, :] for sub in range(2)]
        ot = num[0] * (1.0 / den[0]) - lam * (num[1] * (1.0 / den[1]))
        ms = jnp.mean(ot * ot, axis=0, keepdims=True)
        ot = ot * lax.rsqrt(ms + EPS) * sg_ref[...] * (1.0 - lam_init)
        o_ref[...] = ot.T.astype(o_ref.dtype)


def _prompt_attention(qt, k, vt, lam_vecs, subln_col, *, tq, tk, lam_init):
    t = k.shape[0]
    tq = min(tq, t)
    tk = min(tk, t)
    qc = min(MXU_DIM, tq)
    assert t % tq == 0 and tq % tk == 0 and tq % qc == 0
    steps = [(a, b) for a in range(t // tq) for b in range((a * tq + tq - 1) // tk + 1)]
    qi_arr = jnp.asarray([a for a, _ in steps], jnp.int32)
    kj_arr = jnp.asarray([b for _, b in steps], jnp.int32)
    blocks = (2 * (2 * _nbytes((tq, V_DIM), BF16) + 2 * _nbytes((tk, V_ROWS), BF16))
              + 2 * _nbytes((tq, V_DIM), BF16) + 2 * _nbytes((SUBLANES, tq), F32) + 2 * _nbytes((tq, V_ROWS), F32)
              + 16 * _nbytes((tk, qc), F32))
    grid_spec = pltpu.PrefetchScalarGridSpec(
        num_scalar_prefetch=2,
        grid=(N_HEADS, len(steps)),
        in_specs=[pl.BlockSpec((4, HEAD_DIM), lambda h, s, qi, kj: (0, 0)),
                  pl.BlockSpec((V_DIM, 1), lambda h, s, qi, kj: (0, 0)),
                  pl.BlockSpec((V_DIM, tq), lambda h, s, qi, kj: (h, qi[s])),
                  pl.BlockSpec((tk, V_DIM), lambda h, s, qi, kj: (kj[s], h)),
                  pl.BlockSpec((None, V_ROWS, tk), lambda h, s, qi, kj: (h, 0, kj[s]))],
        out_specs=pl.BlockSpec((tq, V_DIM), lambda h, s, qi, kj: (qi[s], h)),
        scratch_shapes=[pltpu.VMEM((2, V_DIM, tq), BF16),
                        pltpu.VMEM((2, 1, tq), F32),
                        pltpu.VMEM((2, V_ROWS, tq), F32)])
    return pl.pallas_call(
        functools.partial(_attn_kernel, tq=tq, tk=tk, qc=qc, lam_init=lam_init),
        out_shape=jax.ShapeDtypeStruct((t, ATTN_WIDTH), BF16),
        grid_spec=grid_spec,
        compiler_params=pltpu.CompilerParams(dimension_semantics=("parallel", "arbitrary"),
                                             vmem_limit_bytes=_vmem_limit(blocks)),
        name="prompt_attn",
    )(qi_arr, kj_arr, lam_vecs, subln_col, qt, k, vt)


def _reduce_positions(x, op):
    n = x.shape[0]
    if n > REDUCE_WAYS and n % REDUCE_WAYS == 0:
        x = op(x.reshape(n // REDUCE_WAYS, REDUCE_WAYS, *x.shape[1:]), axis=0)
    return op(x, axis=0)


def _absorb_page(q, red_ref, k_ref, v_ref, state):
    m, l, acc = list(state[0:2]), list(state[2:4]), list(state[4:6])
    prod = (k_ref[...] * q[None]).reshape(PAGE_SIZE * N_HEADS, V_DIM).astype(BF16)
    scores = _dot(prod, red_ref[...])
    v = v_ref[...]
    for sub in range(2):
        s = scores[:, sub * LANES:(sub + 1) * LANES].reshape(PAGE_SIZE, N_HEADS, LANES)
        m_new = jnp.maximum(m[sub], _reduce_positions(s, jnp.max))
        alpha = jnp.exp2(m[sub] - m_new)
        p = jnp.exp2(s - m_new[None])
        l[sub] = alpha * l[sub] + _reduce_positions(p, jnp.sum)
        acc[sub] = alpha * acc[sub] + _reduce_positions(p * v, jnp.sum)
        m[sub] = m_new
    return m + l + acc


def _decode_finish_kernel(lam_ref, sg_ref, red_ref, q_ref, kn_ref, vn_ref, st_ref, o_ref, *, lam_init):
    rows = q_ref.shape[0] * N_HEADS
    q = q_ref[...].reshape(rows, V_DIM)
    prod = (kn_ref[...].reshape(rows, V_DIM) * q).astype(BF16)
    s_all = _dot(prod, red_ref[...])
    vn = vn_ref[...].reshape(rows, V_DIM)
    lam = _lambda_value(lam_ref, lam_init)
    outs = []
    for sub in range(2):
        s = s_all[:, sub * LANES:(sub + 1) * LANES]
        m_old = st_ref[:, sub].reshape(rows, LANES)
        l_old = st_ref[:, 2 + sub].reshape(rows, LANES)
        acc_old = st_ref[:, 4 + sub].reshape(rows, V_DIM)
        m_new = jnp.maximum(m_old, s)
        alpha = jnp.exp2(m_old - m_new)
        p = jnp.exp2(s - m_new)
        outs.append((alpha * acc_old + p * vn) / (alpha * l_old + p))
    o = outs[0] - lam * outs[1]
    o_ref[...] = _subln(o, sg_ref[...], lam_init)


def _decode_finish(q, k_new, v_new, state, lam_vecs, subln_g, red, *, lam_init):
    db = q.shape[0]
    full = lambda a: pl.BlockSpec(a.shape, lambda i: (0,) * a.ndim)
    args = (lam_vecs, subln_g, red, q, k_new, v_new, state)
    return pl.pallas_call(
        functools.partial(_decode_finish_kernel, lam_init=lam_init),
        out_shape=jax.ShapeDtypeStruct((db * N_HEADS, V_DIM), F32),
        grid=(1,),
        in_specs=[full(a) for a in args],
        out_specs=pl.BlockSpec((db * N_HEADS, V_DIM), lambda i: (0, 0)),
        name="decode_finish",
    )(*args)


def _decode_split(db, n_pages, steps_first, steps_second):
    p1 = math.gcd(DECODE_PAGES_FIRST, n_pages)
    rows1 = min(db, steps_first // (n_pages // p1))
    rows2 = db - rows1
    p2 = p1
    while rows2 * (n_pages // p2) <= steps_second // 2 and p2 > 1 and n_pages % (p2 // 2) == 0:
        p2 //= 2
    assert rows2 * (n_pages // p2) <= steps_second, "prompt FFN grids too short to carry the decode attention"
    return (rows1, p1), (rows2, p2)


def _mix_out_tail(y, cb_ref, lng_ref, lnb_ref, wco_ref, bco_ref, o_ref, wao_ref, ga_ref, gb_ref, h_ref, g_ref,
                  wout_ref, out_ref):
    y = y + cb_ref[...]
    mu = jnp.mean(y, axis=-1, keepdims=True)
    yc = y - mu
    var = jnp.mean(yc * yc, axis=-1, keepdims=True)
    cv = yc * lax.rsqrt(var + EPS) * lng_ref[...] + lnb_ref[...]
    cv = (cv * _sigmoid(cv)).astype(BF16)
    b_out = _dot(cv, wco_ref[...]) + bco_ref[...]
    a_out = _dot(o_ref[...].astype(BF16), wao_ref[...])
    mrg = (ga_ref[...] * a_out + gb_ref[...] * b_out).astype(BF16)
    out_ref[...] = h_ref[...] + g_ref[...] * _dot(mrg, wout_ref[...])


def _mix_out_prompt_kernel(glu_ref, halo_ref, cw_ref, cb_ref, lng_ref, lnb_ref, wco_ref, bco_ref, o_ref, wao_ref,
                           ga_ref, gb_ref, h_ref, g_ref, wout_ref, out_ref, ext_ref, y_ref, *, tm):
    i = pl.program_id(0)
    c = glu_ref.shape[1]

    @pl.when(i == 0)
    def _():
        ext_ref[0:CONV_HALO, :] = jnp.zeros((CONV_HALO, c), F32)

    @pl.when(i > 0)
    def _():
        ext_ref[0:CONV_HALO, :] = halo_ref[...]

    ext_ref[CONV_HALO:, :] = glu_ref[...]
    rc = min(CONV_ROW_CHUNK, tm)
    off = CONV_HALO - CONV_STATE
    for r0 in range(0, tm, rc):
        for c0 in range(0, c, LANES):
            acc = jnp.zeros((rc, LANES), F32)
            for j in range(CONV_WIDTH):
                acc = acc + cw_ref[j:j + 1, c0:c0 + LANES] * ext_ref[r0 + off + j:r0 + off + j + rc, c0:c0 + LANES]
            y_ref[r0:r0 + rc, c0:c0 + LANES] = acc
    _mix_out_tail(y_ref[...], cb_ref, lng_ref, lnb_ref, wco_ref, bco_ref, o_ref, wao_ref, ga_ref, gb_ref, h_ref,
                  g_ref, wout_ref, out_ref)


def _mix_out_sample_kernel(win_ref, cw_ref, cb_ref, lng_ref, lnb_ref, wco_ref, bco_ref, o_ref, wao_ref,
                           ga_ref, gb_ref, h_ref, g_ref, wout_ref, out_ref):
    y = cw_ref[0:1, :] * win_ref[0]
    for j in range(1, CONV_WIDTH):
        y = y + cw_ref[j:j + 1, :] * win_ref[j]
    _mix_out_tail(y, cb_ref, lng_ref, lnb_ref, wco_ref, bco_ref, o_ref, wao_ref, ga_ref, gb_ref, h_ref, g_ref,
                  wout_ref, out_ref)


def _resident(shape):
    return pl.BlockSpec(shape, lambda i: (0,) * len(shape), pipeline_mode=pl.Buffered(1))


def _mix_out_common_specs(tm, c, d, aw, gate):
    row = lambda i: (i, 0)
    g_spec = (pl.BlockSpec((1, d), lambda i: (0, 0)) if gate.shape[0] == 1 else pl.BlockSpec((tm, d), row))
    return [_resident((CONV_HALO, c)), _resident((1, c)), _resident((1, c)), _resident((1, c)),
            _resident((c, d)), _resident((1, d)),
            pl.BlockSpec((tm, aw), row), _resident((aw, d)),
            pl.BlockSpec((tm, d), row), pl.BlockSpec((tm, d), row), pl.BlockSpec((tm, d), row), g_spec,
            _resident((d, d))]


def _mix_out_bytes(tm, c, d, aw, o_dtype):
    return (2 * (_nbytes((tm, aw), o_dtype) + 4 * _nbytes((tm, d), F32) + _nbytes((tm, d), F32))
            + _nbytes((c, d), BF16) + _nbytes((aw, d), BF16) + _nbytes((d, d), BF16)
            + 8 * _nbytes((tm, d), F32))


def _mix_out_prompt(glu, conv_w, conv_b, ln_g, ln_b, w_co, b_co, o, w_ao, ga, gb, h, gate, w_out, *, tm):
    m, c = glu.shape
    d = h.shape[1]
    aw = o.shape[1]
    tm = min(tm, m)
    assert m % tm == 0 and tm % CONV_HALO == 0
    hb = tm // CONV_HALO
    blocks = _mix_out_bytes(tm, c, d, aw, o.dtype) + 4 * _nbytes((tm + CONV_HALO, c), F32)
    return pl.pallas_call(
        functools.partial(_mix_out_prompt_kernel, tm=tm),
        out_shape=jax.ShapeDtypeStruct((m, d), F32),
        grid=(m // tm,),
        in_specs=[pl.BlockSpec((tm, c), lambda i: (i, 0)),
                  pl.BlockSpec((CONV_HALO, c), lambda i: (jnp.maximum(i * hb - 1, 0), 0))]
                 + _mix_out_common_specs(tm, c, d, aw, gate),
        out_specs=pl.BlockSpec((tm, d), lambda i: (i, 0)),
        scratch_shapes=[pltpu.VMEM((tm + CONV_HALO, c), F32), pltpu.VMEM((tm, c), F32)],
        compiler_params=pltpu.CompilerParams(dimension_semantics=("arbitrary",),
                                             vmem_limit_bytes=_vmem_limit(blocks)),
        name="mix_out_prompt",
    )(glu, glu, conv_w, conv_b, ln_g, ln_b, w_co, b_co, o, w_ao, ga, gb, h, gate, w_out)


def _mix_out_sample(win, conv_w, conv_b, ln_g, ln_b, w_co, b_co, o, w_ao, ga, gb, h, gate, w_out):
    _, m, c = win.shape
    d = h.shape[1]
    aw = o.shape[1]
    blocks = _mix_out_bytes(m, c, d, aw, o.dtype) + 2 * _nbytes(win.shape, F32)
    return pl.pallas_call(
        _mix_out_sample_kernel,
        out_shape=jax.ShapeDtypeStruct((m, d), F32),
        grid=(1,),
        in_specs=[pl.BlockSpec(win.shape, lambda i: (0, 0, 0))] + _mix_out_common_specs(m, c, d, aw, gate),
        out_specs=pl.BlockSpec((m, d), lambda i: (0, 0)),
        compiler_params=pltpu.CompilerParams(dimension_semantics=("arbitrary",),
                                             vmem_limit_bytes=_vmem_limit(blocks)),
        name="mix_out_sample",
    )(win, conv_w, conv_b, ln_g, ln_b, w_co, b_co, o, w_ao, ga, gb, h, gate, w_out)


def _rope_tables(pos):
    half = ROT_DIM // 2
    inv = jnp.power(ROPE_THETA, -jnp.arange(half, dtype=F32) * 2.0 / ROT_DIM)
    ang = pos.astype(F32)[:, None] * inv[None, :]
    cos, sin = jnp.cos(ang), jnp.sin(ang)
    t = pos.shape[0]
    pad = jnp.zeros((t, HEAD_DIM - ROT_DIM), F32)
    zero = jnp.zeros((t, half), F32)
    cos_t = jnp.concatenate([cos, cos, pad + 1.0], axis=1)
    sin_a = jnp.concatenate([-sin, zero, pad], axis=1)
    sin_b = jnp.concatenate([zero, sin, pad], axis=1)
    rep = LANES // HEAD_DIM
    return jnp.tile(cos_t, (1, rep)), jnp.tile(sin_a, (1, rep)), jnp.tile(sin_b, (1, rep))


def _group_mean_matrix(n):
    g = jnp.arange(n) // HEAD_DIM
    return jnp.where(g[:, None] == g[None, :], 1.0 / HEAD_DIM, 0.0).astype(BF16)


def _half_sum_matrix():
    lane_half = jnp.arange(V_DIM) // HEAD_DIM
    col_half = jnp.arange(2 * LANES) // LANES
    return (lane_half[:, None] == col_half[None, :]).astype(BF16)


def _pad_cols(w, mult):
    return jnp.pad(w, ((0, 0), (0, -w.shape[1] % mult)))


def _pad_rows(w, mult):
    return jnp.pad(w, ((0, -w.shape[0] % mult), (0, 0)))


def _tile_config(seq, d_ff):
    return dict(ffn_tm=min(512, seq), ffn_tf=min(512, pl.cdiv(d_ff, LANES) * LANES), mix_in_tm=min(512, seq),
                attn_tq=min(2048, seq), attn_tk=min(512, seq), mix_out_tm=min(256, seq))


def kernel(x_prompt, x_sample, cache_k, cache_v, state_conv, page_table, c_prompt, c_sample, w_ada, b_ada, norm1, w1_gate, w1_up, w1_down, norm2, w_in, q_norm, k_norm, lam_q1, lam_k1, lam_q2, lam_k2, subln, w_attn_out, conv_w, conv_b, conv_ln_g, conv_ln_b, w_conv_out, b_conv_out, w_out, norm3, w2_gate, w2_up, w2_down):
    bsz, seq, d = x_prompt.shape
    db, ts, _ = x_sample.shape
    depth = w_ada.shape[0]
    assert bsz == 1 and ts == 1 and depth == 1, "kernel is specialised to one prompt sequence, one new token, one layer"
    c = conv_w.shape[-1]
    d_ff = w1_gate.shape[-1]
    past_len = page_table.shape[1] * PAGE_SIZE
    cfg = _tile_config(seq, d_ff)
    lam_init = _lambda_init(0)

    tf = cfg["ffn_tf"]
    w1g = _pad_cols(w1_gate[0].astype(BF16), tf)
    w1u = _pad_cols(w1_up[0].astype(BF16), tf)
    w1d = _pad_rows(w1_down[0].astype(BF16), tf)
    w2g = _pad_cols(w2_gate[0].astype(BF16), tf)
    w2u = _pad_cols(w2_up[0].astype(BF16), tf)
    w2d = _pad_rows(w2_down[0].astype(BF16), tf)
    w_in_b = w_in[0].astype(BF16)
    w_ao = w_attn_out[0].astype(BF16)
    w_co = w_conv_out[0].astype(BF16)
    w_o = w_out[0].astype(BF16)

    n_c = bsz + db
    c_all = jnp.concatenate([c_prompt, c_sample, jnp.zeros((-n_c % SUBLANES, d), F32)], axis=0)
    ada = _ada(c_all, w_ada[0], b_ada)
    ada_p = [ada[0:1, i * d:(i + 1) * d] for i in range(N_MOD)]
    ada_s = [ada[1:1 + db, i * d:(i + 1) * d] for i in range(N_MOD)]

    qg = jnp.tile(q_norm[0], c // HEAD_DIM)[None]
    kg = jnp.tile(k_norm[0], c // HEAD_DIM)[None]
    grp = _group_mean_matrix(min(MXU_DIM, c))
    red = _half_sum_matrix()
    lam_vecs = jnp.stack([lam_q1[0], lam_k1[0], lam_q2[0], lam_k2[0]])
    cw = jnp.pad(conv_w[0], ((0, CONV_HALO - CONV_WIDTH), (0, 0)))

    def ffn1(x, mods, tm, decode=None):
        sh1, sc1, g1 = mods[0:3]
        return _ffn(x, norm1, sh1, sc1, g1, w1g, w1u, w1d, tm=tm, tf=tf, decode=decode)

    def ffn2(h, mods, tm, decode=None):
        sh3, sc3, g3 = mods[6:9]
        return _ffn(h, norm3, sh3, sc3, g3, w2g, w2u, w2d, tm=tm, tf=tf, decode=decode)

    def mix_in(h, mods, pos, tm, transposed):
        cos_t, sin_a, sin_b = _rope_tables(pos)
        return _mix_in(h, norm2, mods[3], mods[4], w_in_b, qg, kg, grp, cos_t, sin_a, sin_b, tm=tm,
                       transposed=transposed)

    xs = x_sample[:, 0]
    pos_s = jnp.full((db,), past_len, jnp.int32)
    hs = ffn1(xs, ada_s, db)
    qs, ks, _, vs, _, glu_s, ga_s, gb_s = mix_in(hs, ada_s, pos_s, db, transposed=False)
    qs3, ks3, vs3 = (a.astype(F32).reshape(db, N_HEADS, V_DIM) for a in (qs, ks, vs))

    n_pages = page_table.shape[1]
    ffn_steps = (seq // min(cfg["ffn_tm"], seq)) * (w1g.shape[1] // tf)
    (rows1, pages1), (rows2, pages2) = _decode_split(db, n_pages, ffn_steps, ffn_steps)
    side = dict(page_table=page_table, q=qs3, red=red, cache_k=cache_k.reshape(cache_k.shape[1:]),
                cache_v=cache_v.reshape(cache_v.shape[1:]))
    side1 = dict(side, row0=0, n_rows=rows1, pages=pages1, units_per_row=n_pages // pages1)
    side2 = dict(side, row0=rows1, n_rows=rows2, pages=pages2, units_per_row=n_pages // pages2) if rows2 else None

    xp = x_prompt[0]
    pos_p = jnp.arange(seq, dtype=jnp.int32)
    hp, state1 = ffn1(xp, ada_p, cfg["ffn_tm"], decode=side1)
    qtp, kp, kpb, vp, vtp, glu_p, ga_p, gb_p = mix_in(hp, ada_p, pos_p, cfg["mix_in_tm"], transposed=True)
    op = _prompt_attention(qtp, kpb, vtp, lam_vecs, subln.reshape(V_DIM, 1), tq=cfg["attn_tq"], tk=cfg["attn_tk"],
                           lam_init=lam_init)
    hp2 = _mix_out_prompt(glu_p, cw, conv_b, conv_ln_g, conv_ln_b, w_co, b_conv_out, op, w_ao, ga_p, gb_p, hp,
                          ada_p[5], w_o, tm=cfg["mix_out_tm"])
    if side2 is None:
        yp, state = ffn2(hp2, ada_p, cfg["ffn_tm"]), state1
    else:
        yp, state2 = ffn2(hp2, ada_p, cfg["ffn_tm"], decode=side2)
        state = jnp.concatenate([state1, state2], axis=0)

    os_ = _decode_finish(qs3, ks3, vs3, state, lam_vecs, subln, red, lam_init=lam_init)
    padded_s = jnp.concatenate([state_conv[0], glu_s[:, None, :]], axis=1)
    win = jnp.pad(padded_s.transpose(1, 0, 2), ((0, CONV_HALO - CONV_WIDTH), (0, 0), (0, 0)))
    hs2 = _mix_out_sample(win, cw, conv_b, conv_ln_g, conv_ln_b, w_co, b_conv_out,
                          os_.reshape(db, ATTN_WIDTH), w_ao, ga_s, gb_s, hs, ada_s[5], w_o)
    ys = ffn2(hs2, ada_s, db)

    return (yp[None],
            ys[:, None, :],
            kp.reshape(1, 1, seq, N_HEADS, 2 * HEAD_DIM),
            vp.reshape(1, 1, seq, N_HEADS, V_DIM),
            glu_p[seq - CONV_STATE:][None, None],
            ks.reshape(1, db, 1, N_HEADS, 2 * HEAD_DIM),
            vs.reshape(1, db, 1, N_HEADS, V_DIM),
            padded_s[None, :, 1:, :])
```

```python
import functools
import math

import jax
import jax.numpy as jnp
from jax import lax
from jax.experimental import pallas as pl
from jax.experimental.pallas import tpu as pltpu

N_HEADS = 8
HEAD_DIM = 64
V_DIM = 2 * HEAD_DIM
QK_COLS = N_HEADS * 2 * HEAD_DIM
ATTN_WIDTH = N_HEADS * V_DIM
ROT_DIM = HEAD_DIM // 4
ROPE_THETA = 500000.0
PAGE_SIZE = 128
CONV_WIDTH = 31
CONV_STATE = CONV_WIDTH - 1
EPS = 1e-6
NEG_INF = -1e30
N_MOD = 9
QK_SCALE_LOG2 = HEAD_DIM ** -0.5 * math.log2(math.e)
V_ROWS = V_DIM + 16

LANES = 128
SUBLANES = 8
MXU_DIM = 256
VMEM_BYTES_V7X = 64 * 1024 * 1024
VMEM_LIMIT_CAP = VMEM_BYTES_V7X - 6 * 1024 * 1024

CONV_HALO = 32
CONV_ROW_CHUNK = 128
MIX_IN_ROW_CHUNK = 256
DECODE_PAGES_PER_STEP = 8
DECODE_STATE_ROWS = 6
REDUCE_WAYS = 8

F32 = jnp.float32
BF16 = jnp.bfloat16


def _lambda_init(layer_idx):
    return 0.8 - 0.6 * math.exp(-0.3 * layer_idx)


def _vmem_limit(block_bytes):
    return int(min(VMEM_LIMIT_CAP, block_bytes + block_bytes // 4 + (4 << 20)))


def _nbytes(shape, dtype):
    return math.prod(shape) * jnp.dtype(dtype).itemsize


def _dot(a, b):
    return jnp.dot(a, b, preferred_element_type=F32)


def _sigmoid(x):
    return 1.0 / (1.0 + jnp.exp(-x))


def _rmsnorm_mod(x, gain, shift, scale):
    ms = jnp.mean(x * x, axis=-1, keepdims=True)
    n = x * lax.rsqrt(ms + EPS) * gain
    return n * (1.0 + scale) + shift


def _ada_kernel(c_ref, w_ref, b_ref, o_ref):
    c = c_ref[...]
    a = (c * _sigmoid(c)).astype(BF16)
    o_ref[...] = _dot(a, w_ref[...].astype(BF16)) + b_ref[...]


def _ada(c, w, b):
    m, d = c.shape
    n = w.shape[1]
    tn = d // 2
    assert n % tn == 0 and tn % LANES == 0
    blocks = 2 * (_nbytes((m, d), F32) + _nbytes((d, tn), F32) + _nbytes((m, tn), F32))
    return pl.pallas_call(
        _ada_kernel,
        out_shape=jax.ShapeDtypeStruct((m, n), F32),
        grid=(n // tn,),
        in_specs=[pl.BlockSpec((m, d), lambda j: (0, 0)),
                  pl.BlockSpec((d, tn), lambda j: (0, j)),
                  pl.BlockSpec((1, tn), lambda j: (0, j))],
        out_specs=pl.BlockSpec((m, tn), lambda j: (0, j)),
        compiler_params=pltpu.CompilerParams(dimension_semantics=("arbitrary",),
                                             vmem_limit_bytes=_vmem_limit(blocks)),
        name="ada",
    )(c, w, b)


def _ffn_kernel(x_ref, ng_ref, sh_ref, sc_ref, g_ref, wg_ref, wu_ref, wd_ref, o_ref, n_ref, acc_ref, *, d_ff):
    j = pl.program_id(1)
    tf = wg_ref.shape[1]

    @pl.when(j == 0)
    def _():
        n = _rmsnorm_mod(x_ref[...], ng_ref[...], sh_ref[...], sc_ref[...])
        n_ref[...] = n.astype(BF16)
        acc_ref[...] = jnp.zeros_like(acc_ref)

    n = n_ref[...]
    a = _dot(n, wg_ref[...])
    u = _dot(n, wu_ref[...])
    hmid = a * _sigmoid(a) * u
    wd = wd_ref[...]
    if d_ff % tf:
        valid = d_ff - j * tf
        hmid = jnp.where(lax.broadcasted_iota(jnp.int32, hmid.shape, 1) < valid, hmid, 0.0)
        wd = jnp.where(lax.broadcasted_iota(jnp.int32, wd.shape, 0) < valid, wd, jnp.zeros_like(wd))
    acc_ref[...] += _dot(hmid.astype(BF16), wd)

    @pl.when(j == pl.num_programs(1) - 1)
    def _():
        o_ref[...] = x_ref[...] + 0.5 * g_ref[...] * acc_ref[...]


def _mod_spec(mod, tm, d):
    if mod.shape[0] == 1:
        return pl.BlockSpec((1, d), lambda i, j: (0, 0))
    return pl.BlockSpec((tm, d), lambda i, j: (i, 0))


def _ffn(x, norm_g, shift, scale, gate, wg, wu, wd, *, tm, tf):
    m, d = x.shape
    d_ff = wg.shape[1]
    tm = min(tm, m)
    assert m % tm == 0
    blocks = (2 * (2 * _nbytes((tm, d), F32) + 2 * _nbytes((d, tf), BF16) + _nbytes((tf, d), BF16))
              + _nbytes((tm, d), BF16) + _nbytes((tm, d), F32) + 2 * 3 * _nbytes((min(shift.shape[0], tm), d), F32))
    return pl.pallas_call(
        functools.partial(_ffn_kernel, d_ff=d_ff),
        out_shape=jax.ShapeDtypeStruct((m, d), F32),
        grid=(m // tm, pl.cdiv(d_ff, tf)),
        in_specs=[pl.BlockSpec((tm, d), lambda i, j: (i, 0)),
                  pl.BlockSpec((1, d), lambda i, j: (0, 0)),
                  _mod_spec(shift, tm, d), _mod_spec(scale, tm, d), _mod_spec(gate, tm, d),
                  pl.BlockSpec((d, tf), lambda i, j: (0, j)),
                  pl.BlockSpec((d, tf), lambda i, j: (0, j)),
                  pl.BlockSpec((tf, d), lambda i, j: (j, 0))],
        out_specs=pl.BlockSpec((tm, d), lambda i, j: (i, 0)),
        scratch_shapes=[pltpu.VMEM((tm, d), BF16), pltpu.VMEM((tm, d), F32)],
        compiler_params=pltpu.CompilerParams(dimension_semantics=("parallel", "arbitrary"),
                                             vmem_limit_bytes=_vmem_limit(blocks)),
        name="ffn",
    )(x, norm_g, shift, scale, gate, wg, wu, wd)


def _subhead_rms_rope(z, gain, grp, cos_t, sin_a, sin_b):
    tn = z.shape[1]
    zz = (z * z).astype(BF16)
    ms = jnp.concatenate([_dot(zz[:, c:c + MXU_DIM], grp) for c in range(0, tn, MXU_DIM)], axis=1)
    y = z * lax.rsqrt(ms + EPS) * gain
    outs = []
    for c in range(0, tn, LANES):
        yc = y[:, c:c + LANES]
        outs.append(yc * cos_t + pltpu.roll(yc, LANES - ROT_DIM // 2, 1) * sin_a
                    + pltpu.roll(yc, ROT_DIM // 2, 1) * sin_b)
    return jnp.concatenate(outs, axis=1)


def _mix_in_kernel(h_ref, ng_ref, sh_ref, sc_ref, w_ref, qg_ref, kg_ref, grp_ref, cos_ref, sa_ref, sb_ref,
                   q_ref, k_ref, kb_ref, v_ref, vb_ref, glu_ref, ga_ref, gb_ref, u_ref, zc_ref, *, nq, transposed):
    j = pl.program_id(1)

    tm = u_ref.shape[0]
    rc = min(MIX_IN_ROW_CHUNK, tm)

    def column_group(lo, hi, epilogue):
        @pl.when((j >= lo) & (j < hi))
        def _():
            for r0 in range(0, tm, rc):
                rows = slice(r0, r0 + rc)
                epilogue(rows, _dot(u_ref[rows, :], w_ref[...]))

    def rope(z, rows, gain_ref):
        return _subhead_rms_rope(z, gain_ref[...], grp_ref[...], cos_ref[rows, :], sa_ref[rows, :], sb_ref[rows, :])

    def q_epilogue(rows, z):
        r = rope(z, rows, qg_ref) * QK_SCALE_LOG2
        if transposed:
            q_ref[:, rows] = r.T.astype(BF16)
        else:
            q_ref[rows, :] = r.astype(BF16)

    def k_epilogue(rows, z):
        r = rope(z, rows, kg_ref)
        k_ref[rows, :] = r
        kb_ref[rows, :] = r.astype(BF16)

    def v_epilogue(rows, z):
        v_ref[rows, :] = z
        if transposed:
            heads = vb_ref.shape[0]
            vb_ref[:, 0:V_DIM, rows] = z.T.reshape(heads, V_DIM, rc).astype(BF16)
            vb_ref[:, V_DIM:, rows] = jnp.ones((heads, V_ROWS - V_DIM, rc), BF16)
        else:
            vb_ref[rows, :] = z.astype(BF16)

    def zc_epilogue(rows, z):
        zc_ref[rows, :] = z

    def glu_epilogue(rows, z):
        glu_ref[rows, :] = zc_ref[rows, :] * _sigmoid(z)

    def ga_epilogue(rows, z):
        ga_ref[rows, :] = _sigmoid(z)

    def gb_epilogue(rows, z):
        gb_ref[rows, :] = _sigmoid(z)

    @pl.when(j == 0)
    def _():
        u_ref[...] = _rmsnorm_mod(h_ref[...], ng_ref[...], sh_ref[...], sc_ref[...]).astype(BF16)

    column_group(0, nq, q_epilogue)
    column_group(nq, 2 * nq, k_epilogue)
    column_group(2 * nq, 3 * nq, v_epilogue)
    column_group(3 * nq, 3 * nq + 1, zc_epilogue)
    column_group(3 * nq + 1, 3 * nq + 2, glu_epilogue)
    column_group(3 * nq + 2, 3 * nq + 4, ga_epilogue)
    column_group(3 * nq + 4, 3 * nq + 6, gb_epilogue)


def _mix_in(h, norm_g, shift, scale, w_in, q_gain, k_gain, grp, cos_t, sin_a, sin_b, *, tm, transposed):
    m, d = h.shape
    tn = d // 2
    nq = QK_COLS // tn
    assert QK_COLS % tn == 0 and tn % LANES == 0
    ncol = w_in.shape[1] // tn
    assert ncol == 3 * nq + 6
    tm = min(tm, m)
    assert m % tm == 0

    def col(lo):
        return lambda i, j: (i, jnp.clip(j - lo, 0, nq - 1))

    def col2(lo):
        return lambda i, j: (i, jnp.clip(j - lo, 0, 1))

    row = lambda i, j: (i, 0)
    const = lambda i, j: (0, 0)
    blocks = (2 * (_nbytes((tm, d), F32) + _nbytes((d, tn), BF16) + 3 * _nbytes((tm, LANES), F32)
                   + 6 * _nbytes((tm, tn), F32) + 3 * _nbytes((tm, tn), BF16))
              + _nbytes((tm, d), BF16) + _nbytes((tm, tn), F32) + 2 * 2 * _nbytes((min(shift.shape[0], tm), d), F32))
    gsz = grp.shape[0]
    if transposed:
        assert tm % LANES == 0 and tn % V_DIM == 0
        q_shape = jax.ShapeDtypeStruct((QK_COLS, m), BF16)
        q_spec = pl.BlockSpec((tn, tm), lambda i, j: (jnp.clip(j, 0, nq - 1), i))
        v_shape = jax.ShapeDtypeStruct((N_HEADS, V_ROWS, m), BF16)
        v_spec = pl.BlockSpec((tn // V_DIM, V_ROWS, tm), lambda i, j: (jnp.clip(j - 2 * nq, 0, nq - 1), 0, i))
    else:
        q_shape = jax.ShapeDtypeStruct((m, QK_COLS), BF16)
        q_spec = pl.BlockSpec((tm, tn), col(0))
        v_shape = jax.ShapeDtypeStruct((m, ATTN_WIDTH), BF16)
        v_spec = pl.BlockSpec((tm, tn), col(2 * nq))
    outs = pl.pallas_call(
        functools.partial(_mix_in_kernel, nq=nq, transposed=transposed),
        out_shape=[q_shape,
                   jax.ShapeDtypeStruct((m, QK_COLS), F32),
                   jax.ShapeDtypeStruct((m, QK_COLS), BF16),
                   jax.ShapeDtypeStruct((m, ATTN_WIDTH), F32),
                   v_shape,
                   jax.ShapeDtypeStruct((m, tn), F32),
                   jax.ShapeDtypeStruct((m, d), F32),
                   jax.ShapeDtypeStruct((m, d), F32)],
        grid=(m // tm, ncol),
        in_specs=[pl.BlockSpec((tm, d), row),
                  pl.BlockSpec((1, d), const),
                  _mod_spec(shift, tm, d), _mod_spec(scale, tm, d),
                  pl.BlockSpec((d, tn), lambda i, j: (0, j)),
                  pl.BlockSpec((1, tn), const), pl.BlockSpec((1, tn), const),
                  pl.BlockSpec((gsz, gsz), const),
                  pl.BlockSpec((tm, LANES), row), pl.BlockSpec((tm, LANES), row), pl.BlockSpec((tm, LANES), row)],
        out_specs=[q_spec,
                   pl.BlockSpec((tm, tn), col(nq)), pl.BlockSpec((tm, tn), col(nq)),
                   pl.BlockSpec((tm, tn), col(2 * nq)), v_spec,
                   pl.BlockSpec((tm, tn), row),
                   pl.BlockSpec((tm, tn), col2(3 * nq + 2)),
                   pl.BlockSpec((tm, tn), col2(3 * nq + 4))],
        scratch_shapes=[pltpu.VMEM((tm, d), BF16), pltpu.VMEM((tm, tn), F32)],
        compiler_params=pltpu.CompilerParams(dimension_semantics=("parallel", "arbitrary"),
                                             vmem_limit_bytes=_vmem_limit(blocks)),
        name="mix_in",
    )(h, norm_g, shift, scale, w_in, q_gain, k_gain, grp, cos_t, sin_a, sin_b)
    return outs


def _lambda_value(lam_ref, lam_init):
    lv = lam_ref[...]
    l1 = jnp.sum(lv[0:1] * lv[1:2], axis=-1, keepdims=True)
    l2 = jnp.sum(lv[2:3] * lv[3:4], axis=-1, keepdims=True)
    return jnp.exp(l1) - jnp.exp(l2) + lam_init


def _subln(o, gain, lam_init):
    ms = jnp.mean(o * o, axis=-1, keepdims=True)
    return o * lax.rsqrt(ms + EPS) * gain * (1.0 - lam_init)


def _attn_kernel(qi_ref, kj_ref, lam_ref, sg_ref, qt_ref, k_ref, vt_ref, o_ref,
                 qz_ref, m_ref, acc_ref, *, tq, tk, qc, lam_init):
    s_idx = pl.program_id(1)
    qi = qi_ref[s_idx]
    kj = kj_ref[s_idx]
    q_lo = qi * tq
    k_lo = kj * tk
    last_kj = (q_lo + tq - 1) // tk

    @pl.when(kj == 0)
    def _():
        qt = qt_ref[...]
        row = lax.broadcasted_iota(jnp.int32, qt.shape, 0)
        zero = jnp.zeros_like(qt)
        qz_ref[0] = jnp.where(row < HEAD_DIM, qt, zero)
        qz_ref[1] = jnp.where(row >= HEAD_DIM, qt, zero)
        m_ref[...] = jnp.full_like(m_ref, NEG_INF)
        acc_ref[...] = jnp.zeros_like(acc_ref)

    def step(delta):
        k = k_ref[...]
        vt = vt_ref[...]
        chunks = []
        for c0 in range(0, tq, qc):
            if delta is None or c0 >= delta + tk - 1:
                chunks.append((c0, False))
            elif c0 + qc - 1 >= delta:
                chunks.append((c0, True))

        def scores(c0):
            return [_dot(k, qz_ref[sub, :, c0:c0 + qc]) for sub in range(2)]

        nxt = scores(chunks[0][0])
        for idx, (c0, masked) in enumerate(chunks):
            cur = nxt
            if idx + 1 < len(chunks):
                nxt = scores(chunks[idx + 1][0])
            cols = slice(c0, c0 + qc)
            if masked:
                q_pos = c0 + lax.broadcasted_iota(jnp.int32, (tk, qc), 1)
                k_pos = delta + lax.broadcasted_iota(jnp.int32, (tk, qc), 0)
                keep = q_pos >= k_pos
            probs, alphas, m_news = [], [], []
            for sub in range(2):
                st = jnp.where(keep, cur[sub], NEG_INF) if masked else cur[sub]
                m_old = m_ref[sub, :, cols]
                m_new = jnp.maximum(m_old, jnp.max(st, axis=0, keepdims=True))
                alphas.append(jnp.exp2(m_old - m_new))
                probs.append(jnp.exp2(st - m_new).astype(BF16))
                m_news.append(m_new)
            for sub in range(2):
                acc_ref[sub, :, cols] = alphas[sub] * acc_ref[sub, :, cols] + _dot(vt, probs[sub])
                m_ref[sub, :, cols] = m_news[sub]

    rel = k_lo - q_lo

    @pl.when(rel < 0)
    def _():
        step(None)

    for delta in range(0, tq, tk):
        @pl.when(rel == delta)
        def _(delta=delta):
            step(delta)

    @pl.when(kj == last_kj)
    def _():
        lam = _lambda_value(lam_ref, lam_init)
        num = [acc_ref[sub, 0:V_DIM, :] for sub in range(2)]
        den = [acc_ref[sub, V_DIM:V_DIM + 1, :] for sub in range(2)]
        ot = num[0] * (1.0 / den[0]) - lam * (num[1] * (1.0 / den[1]))
        ms = jnp.mean(ot * ot, axis=0, keepdims=True)
        ot = ot * lax.rsqrt(ms + EPS) * sg_ref[...] * (1.0 - lam_init)
        o_ref[...] = ot.T.astype(o_ref.dtype)


def _prompt_attention(qt, k, vt, lam_vecs, subln_col, *, tq, tk, lam_init):
    t = k.shape[0]
    tq = min(tq, t)
    tk = min(tk, t)
    qc = min(MXU_DIM, tq)
    assert t % tq == 0 and tq % tk == 0 and tq % qc == 0
    steps = [(a, b) for a in range(t // tq) for b in range((a * tq + tq - 1) // tk + 1)]
    qi_arr = jnp.asarray([a for a, _ in steps], jnp.int32)
    kj_arr = jnp.asarray([b for _, b in steps], jnp.int32)
    blocks = (2 * (2 * _nbytes((tq, V_DIM), BF16) + 2 * _nbytes((tk, V_ROWS), BF16))
              + 2 * _nbytes((tq, V_DIM), BF16) + 2 * _nbytes((SUBLANES, tq), F32) + 2 * _nbytes((tq, V_ROWS), F32)
              + 16 * _nbytes((tk, qc), F32))
    grid_spec = pltpu.PrefetchScalarGridSpec(
        num_scalar_prefetch=2,
        grid=(N_HEADS, len(steps)),
        in_specs=[pl.BlockSpec((4, HEAD_DIM), lambda h, s, qi, kj: (0, 0)),
                  pl.BlockSpec((V_DIM, 1), lambda h, s, qi, kj: (0, 0)),
                  pl.BlockSpec((V_DIM, tq), lambda h, s, qi, kj: (h, qi[s])),
                  pl.BlockSpec((tk, V_DIM), lambda h, s, qi, kj: (kj[s], h)),
                  pl.BlockSpec((None, V_ROWS, tk), lambda h, s, qi, kj: (h, 0, kj[s]))],
        out_specs=pl.BlockSpec((tq, V_DIM), lambda h, s, qi, kj: (qi[s], h)),
        scratch_shapes=[pltpu.VMEM((2, V_DIM, tq), BF16),
                        pltpu.VMEM((2, 1, tq), F32),
                        pltpu.VMEM((2, V_ROWS, tq), F32)])
    return pl.pallas_call(
        functools.partial(_attn_kernel, tq=tq, tk=tk, qc=qc, lam_init=lam_init),
        out_shape=jax.ShapeDtypeStruct((t, ATTN_WIDTH), BF16),
        grid_spec=grid_spec,
        compiler_params=pltpu.CompilerParams(dimension_semantics=("parallel", "arbitrary"),
                                             vmem_limit_bytes=_vmem_limit(blocks)),
        name="prompt_attn",
    )(qi_arr, kj_arr, lam_vecs, subln_col, qt, k, vt)


def _reduce_positions(x, op):
    n = x.shape[0]
    if n > REDUCE_WAYS and n % REDUCE_WAYS == 0:
        x = op(x.reshape(n // REDUCE_WAYS, REDUCE_WAYS, *x.shape[1:]), axis=0)
    return op(x, axis=0)


def _page_scores(q, red_ref, k_ref):
    prod = (k_ref[...] * q[None]).reshape(PAGE_SIZE * N_HEADS, V_DIM).astype(BF16)
    return _dot(prod, red_ref[...])


def _page_softmax(scores, v_ref, state):
    m, l, acc = list(state[0:2]), list(state[2:4]), list(state[4:6])
    v = v_ref[...]
    for sub in range(2):
        s = scores[:, sub * LANES:(sub + 1) * LANES].reshape(PAGE_SIZE, N_HEADS, LANES)
        m_new = jnp.maximum(m[sub], _reduce_positions(s, jnp.max))
        alpha = jnp.exp2(m[sub] - m_new)
        p = jnp.exp2(s - m_new[None])
        l[sub] = alpha * l[sub] + _reduce_positions(p, jnp.sum)
        acc[sub] = alpha * acc[sub] + _reduce_positions(p * v, jnp.sum)
        m[sub] = m_new
    return m + l + acc


def _decode_finish_kernel(lam_ref, sg_ref, red_ref, q_ref, kn_ref, vn_ref, st_ref, o_ref, *, lam_init):
    rows = q_ref.shape[0] * N_HEADS
    q = q_ref[...].reshape(rows, V_DIM)
    prod = (kn_ref[...].reshape(rows, V_DIM) * q).astype(BF16)
    s_all = _dot(prod, red_ref[...])
    vn = vn_ref[...].reshape(rows, V_DIM)
    lam = _lambda_value(lam_ref, lam_init)
    outs = []
    for sub in range(2):
        s = s_all[:, sub * LANES:(sub + 1) * LANES]
        m_old = st_ref[:, sub].reshape(rows, LANES)
        l_old = st_ref[:, 2 + sub].reshape(rows, LANES)
        acc_old = st_ref[:, 4 + sub].reshape(rows, V_DIM)
        m_new = jnp.maximum(m_old, s)
        alpha = jnp.exp2(m_old - m_new)
        p = jnp.exp2(s - m_new)
        outs.append((alpha * acc_old + p * vn) / (alpha * l_old + p))
    o = outs[0] - lam * outs[1]
    o_ref[...] = _subln(o, sg_ref[...], lam_init)


def _decode_finish(q, k_new, v_new, state, lam_vecs, subln_g, red, *, lam_init):
    db = q.shape[0]
    full = lambda a: pl.BlockSpec(a.shape, lambda i: (0,) * a.ndim)
    args = (lam_vecs, subln_g, red, q, k_new, v_new, state)
    return pl.pallas_call(
        functools.partial(_decode_finish_kernel, lam_init=lam_init),
        out_shape=jax.ShapeDtypeStruct((db * N_HEADS, V_DIM), F32),
        grid=(1,),
        in_specs=[full(a) for a in args],
        out_specs=pl.BlockSpec((db * N_HEADS, V_DIM), lambda i: (0, 0)),
        name="decode_finish",
    )(*args)


def _decode_pages_kernel(pt_ref, q_ref, red_ref, *rest, pages):
    k_refs, v_refs = rest[:pages], rest[pages:2 * pages]
    st_out_ref, st_ref = rest[2 * pages:]
    p_idx = pl.program_id(1)

    @pl.when(p_idx == 0)
    def _():
        st_ref[0:2] = jnp.full((2, N_HEADS, LANES), NEG_INF, F32)
        st_ref[2:DECODE_STATE_ROWS] = jnp.zeros((DECODE_STATE_ROWS - 2, N_HEADS, LANES), F32)

    q = q_ref[0]
    state = [st_ref[r] for r in range(DECODE_STATE_ROWS)]
    nxt = _page_scores(q, red_ref, k_refs[0])
    for i in range(pages):
        cur = nxt
        if i + 1 < pages:
            nxt = _page_scores(q, red_ref, k_refs[i + 1])
        state = _page_softmax(cur, v_refs[i], state)
    for r in range(DECODE_STATE_ROWS):
        st_ref[r] = state[r]

    @pl.when(p_idx == pl.num_programs(1) - 1)
    def _():
        st_out_ref[0] = st_ref[...]


def _decode_pages(page_table, q, cache_k, cache_v, red):
    db, n_pages = page_table.shape
    pages = math.gcd(DECODE_PAGES_PER_STEP, n_pages)
    page_blk = (None, PAGE_SIZE, N_HEADS, V_DIM)
    page_bytes = _nbytes((PAGE_SIZE, N_HEADS, V_DIM), F32)
    blocks = 2 * 2 * pages * page_bytes + 12 * page_bytes
    state_blk = (1, DECODE_STATE_ROWS, N_HEADS, LANES)

    def page_spec(i):
        return pl.BlockSpec(page_blk, lambda b, p, pt: (pt[b * n_pages + p * pages + i], 0, 0, 0))

    grid_spec = pltpu.PrefetchScalarGridSpec(
        num_scalar_prefetch=1,
        grid=(db, n_pages // pages),
        in_specs=[pl.BlockSpec((1, N_HEADS, V_DIM), lambda b, p, pt: (b, 0, 0)),
                  pl.BlockSpec((V_DIM, 2 * LANES), lambda b, p, pt: (0, 0))]
                 + [page_spec(i) for i in range(pages)] * 2,
        out_specs=pl.BlockSpec(state_blk, lambda b, p, pt: (b, 0, 0, 0)),
        scratch_shapes=[pltpu.VMEM(state_blk[1:], F32)])
    return pl.pallas_call(
        functools.partial(_decode_pages_kernel, pages=pages),
        out_shape=jax.ShapeDtypeStruct((db,) + state_blk[1:], F32),
        grid_spec=grid_spec,
        compiler_params=pltpu.CompilerParams(dimension_semantics=("parallel", "arbitrary"),
                                             vmem_limit_bytes=_vmem_limit(blocks)),
        name="decode_pages",
    )(page_table.reshape(-1), q, red, *([cache_k] * pages), *([cache_v] * pages))


def _mix_out_tail(y, cb_ref, lng_ref, lnb_ref, wco_ref, bco_ref, o_ref, wao_ref, ga_ref, gb_ref, h_ref, g_ref,
                  wout_ref, out_ref):
    y = y + cb_ref[...]
    mu = jnp.mean(y, axis=-1, keepdims=True)
    yc = y - mu
    var = jnp.mean(yc * yc, axis=-1, keepdims=True)
    cv = yc * lax.rsqrt(var + EPS) * lng_ref[...] + lnb_ref[...]
    cv = (cv * _sigmoid(cv)).astype(BF16)
    b_out = _dot(cv, wco_ref[...]) + bco_ref[...]
    a_out = _dot(o_ref[...].astype(BF16), wao_ref[...])
    mrg = (ga_ref[...] * a_out + gb_ref[...] * b_out).astype(BF16)
    out_ref[...] = h_ref[...] + g_ref[...] * _dot(mrg, wout_ref[...])


def _mix_out_prompt_kernel(glu_ref, halo_ref, cw_ref, cb_ref, lng_ref, lnb_ref, wco_ref, bco_ref, o_ref, wao_ref,
                           ga_ref, gb_ref, h_ref, g_ref, wout_ref, out_ref, ext_ref, y_ref, *, tm):
    i = pl.program_id(0)
    c = glu_ref.shape[1]

    @pl.when(i == 0)
    def _():
        ext_ref[0:CONV_HALO, :] = jnp.zeros((CONV_HALO, c), F32)

    @pl.when(i > 0)
    def _():
        ext_ref[0:CONV_HALO, :] = halo_ref[...]

    ext_ref[CONV_HALO:, :] = glu_ref[...]
    rc = min(CONV_ROW_CHUNK, tm)
    off = CONV_HALO - CONV_STATE
    for r0 in range(0, tm, rc):
        for c0 in range(0, c, LANES):
            acc = jnp.zeros((rc, LANES), F32)
            for j in range(CONV_WIDTH):
                acc = acc + cw_ref[j:j + 1, c0:c0 + LANES] * ext_ref[r0 + off + j:r0 + off + j + rc, c0:c0 + LANES]
            y_ref[r0:r0 + rc, c0:c0 + LANES] = acc
    _mix_out_tail(y_ref[...], cb_ref, lng_ref, lnb_ref, wco_ref, bco_ref, o_ref, wao_ref, ga_ref, gb_ref, h_ref,
                  g_ref, wout_ref, out_ref)


def _mix_out_sample_kernel(win_ref, cw_ref, cb_ref, lng_ref, lnb_ref, wco_ref, bco_ref, o_ref, wao_ref,
                           ga_ref, gb_ref, h_ref, g_ref, wout_ref, out_ref):
    y = cw_ref[0:1, :] * win_ref[0]
    for j in range(1, CONV_WIDTH):
        y = y + cw_ref[j:j + 1, :] * win_ref[j]
    _mix_out_tail(y, cb_ref, lng_ref, lnb_ref, wco_ref, bco_ref, o_ref, wao_ref, ga_ref, gb_ref, h_ref, g_ref,
                  wout_ref, out_ref)


def _resident(shape):
    return pl.BlockSpec(shape, lambda i: (0,) * len(shape), pipeline_mode=pl.Buffered(1))


def _mix_out_common_specs(tm, c, d, aw, gate):
    row = lambda i: (i, 0)
    g_spec = (pl.BlockSpec((1, d), lambda i: (0, 0)) if gate.shape[0] == 1 else pl.BlockSpec((tm, d), row))
    return [_resident((CONV_HALO, c)), _resident((1, c)), _resident((1, c)), _resident((1, c)),
            _resident((c, d)), _resident((1, d)),
            pl.BlockSpec((tm, aw), row), _resident((aw, d)),
            pl.BlockSpec((tm, d), row), pl.BlockSpec((tm, d), row), pl.BlockSpec((tm, d), row), g_spec,
            _resident((d, d))]


def _mix_out_bytes(tm, c, d, aw, o_dtype):
    return (2 * (_nbytes((tm, aw), o_dtype) + 4 * _nbytes((tm, d), F32) + _nbytes((tm, d), F32))
            + _nbytes((c, d), BF16) + _nbytes((aw, d), BF16) + _nbytes((d, d), BF16)
            + 8 * _nbytes((tm, d), F32))


def _mix_out_prompt(glu, conv_w, conv_b, ln_g, ln_b, w_co, b_co, o, w_ao, ga, gb, h, gate, w_out, *, tm):
    m, c = glu.shape
    d = h.shape[1]
    aw = o.shape[1]
    tm = min(tm, m)
    assert m % tm == 0 and tm % CONV_HALO == 0
    hb = tm // CONV_HALO
    blocks = _mix_out_bytes(tm, c, d, aw, o.dtype) + 4 * _nbytes((tm + CONV_HALO, c), F32)
    return pl.pallas_call(
        functools.partial(_mix_out_prompt_kernel, tm=tm),
        out_shape=jax.ShapeDtypeStruct((m, d), F32),
        grid=(m // tm,),
        in_specs=[pl.BlockSpec((tm, c), lambda i: (i, 0)),
                  pl.BlockSpec((CONV_HALO, c), lambda i: (jnp.maximum(i * hb - 1, 0), 0))]
                 + _mix_out_common_specs(tm, c, d, aw, gate),
        out_specs=pl.BlockSpec((tm, d), lambda i: (i, 0)),
        scratch_shapes=[pltpu.VMEM((tm + CONV_HALO, c), F32), pltpu.VMEM((tm, c), F32)],
        compiler_params=pltpu.CompilerParams(dimension_semantics=("arbitrary",),
                                             vmem_limit_bytes=_vmem_limit(blocks)),
        name="mix_out_prompt",
    )(glu, glu, conv_w, conv_b, ln_g, ln_b, w_co, b_co, o, w_ao, ga, gb, h, gate, w_out)


def _mix_out_sample(win, conv_w, conv_b, ln_g, ln_b, w_co, b_co, o, w_ao, ga, gb, h, gate, w_out):
    _, m, c = win.shape
    d = h.shape[1]
    aw = o.shape[1]
    blocks = _mix_out_bytes(m, c, d, aw, o.dtype) + 2 * _nbytes(win.shape, F32)
    return pl.pallas_call(
        _mix_out_sample_kernel,
        out_shape=jax.ShapeDtypeStruct((m, d), F32),
        grid=(1,),
        in_specs=[pl.BlockSpec(win.shape, lambda i: (0, 0, 0))] + _mix_out_common_specs(m, c, d, aw, gate),
        out_specs=pl.BlockSpec((m, d), lambda i: (0, 0)),
        compiler_params=pltpu.CompilerParams(dimension_semantics=("arbitrary",),
                                             vmem_limit_bytes=_vmem_limit(blocks)),
        name="mix_out_sample",
    )(win, conv_w, conv_b, ln_g, ln_b, w_co, b_co, o, w_ao, ga, gb, h, gate, w_out)


def _rope_tables(pos):
    half = ROT_DIM // 2
    inv = jnp.power(ROPE_THETA, -jnp.arange(half, dtype=F32) * 2.0 / ROT_DIM)
    ang = pos.astype(F32)[:, None] * inv[None, :]
    cos, sin = jnp.cos(ang), jnp.sin(ang)
    t = pos.shape[0]
    pad = jnp.zeros((t, HEAD_DIM - ROT_DIM), F32)
    zero = jnp.zeros((t, half), F32)
    cos_t = jnp.concatenate([cos, cos, pad + 1.0], axis=1)
    sin_a = jnp.concatenate([-sin, zero, pad], axis=1)
    sin_b = jnp.concatenate([zero, sin, pad], axis=1)
    rep = LANES // HEAD_DIM
    return jnp.tile(cos_t, (1, rep)), jnp.tile(sin_a, (1, rep)), jnp.tile(sin_b, (1, rep))


def _group_mean_matrix(n):
    g = jnp.arange(n) // HEAD_DIM
    return jnp.where(g[:, None] == g[None, :], 1.0 / HEAD_DIM, 0.0).astype(BF16)


def _half_sum_matrix():
    lane_half = jnp.arange(V_DIM) // HEAD_DIM
    col_half = jnp.arange(2 * LANES) // LANES
    return (lane_half[:, None] == col_half[None, :]).astype(BF16)


def _tile_config(seq, d_ff):
    return dict(ffn_tm=min(512, seq), ffn_tf=min(512, pl.cdiv(d_ff, LANES) * LANES), mix_in_tm=min(512, seq),
                attn_tq=min(4096, seq), attn_tk=min(512, seq), mix_out_tm=min(256, seq))


def kernel(x_prompt, x_sample, cache_k, cache_v, state_conv, page_table, c_prompt, c_sample, w_ada, b_ada, norm1, w1_gate, w1_up, w1_down, norm2, w_in, q_norm, k_norm, lam_q1, lam_k1, lam_q2, lam_k2, subln, w_attn_out, conv_w, conv_b, conv_ln_g, conv_ln_b, w_conv_out, b_conv_out, w_out, norm3, w2_gate, w2_up, w2_down):
    bsz, seq, d = x_prompt.shape
    db, ts, _ = x_sample.shape
    depth = w_ada.shape[0]
    assert bsz == 1 and ts == 1 and depth == 1, "kernel is specialised to one prompt sequence, one new token, one layer"
    c = conv_w.shape[-1]
    d_ff = w1_gate.shape[-1]
    past_len = page_table.shape[1] * PAGE_SIZE
    cfg = _tile_config(seq, d_ff)
    lam_init = _lambda_init(0)

    tf = cfg["ffn_tf"]
    w1g, w1u, w1d = (w[0].astype(BF16) for w in (w1_gate, w1_up, w1_down))
    w2g, w2u, w2d = (w[0].astype(BF16) for w in (w2_gate, w2_up, w2_down))
    w_in_b = w_in[0].astype(BF16)
    w_ao = w_attn_out[0].astype(BF16)
    w_co = w_conv_out[0].astype(BF16)
    w_o = w_out[0].astype(BF16)

    n_c = bsz + db
    c_all = jnp.concatenate([c_prompt, c_sample, jnp.zeros((-n_c % SUBLANES, d), F32)], axis=0)
    ada = _ada(c_all, w_ada[0], b_ada)
    ada_p = [ada[0:1, i * d:(i + 1) * d] for i in range(N_MOD)]
    ada_s = [ada[1:1 + db, i * d:(i + 1) * d] for i in range(N_MOD)]

    qg = jnp.tile(q_norm[0], c // HEAD_DIM)[None]
    kg = jnp.tile(k_norm[0], c // HEAD_DIM)[None]
    grp = _group_mean_matrix(min(MXU_DIM, c))
    red = _half_sum_matrix()
    lam_vecs = jnp.stack([lam_q1[0], lam_k1[0], lam_q2[0], lam_k2[0]])
    cw = jnp.pad(conv_w[0], ((0, CONV_HALO - CONV_WIDTH), (0, 0)))

    def ffn1(x, mods, tm):
        sh1, sc1, g1 = mods[0:3]
        return _ffn(x, norm1, sh1, sc1, g1, w1g, w1u, w1d, tm=tm, tf=tf)

    def ffn2(h, mods, tm):
        sh3, sc3, g3 = mods[6:9]
        return _ffn(h, norm3, sh3, sc3, g3, w2g, w2u, w2d, tm=tm, tf=tf)

    def mix_in(h, mods, pos, tm, transposed):
        cos_t, sin_a, sin_b = _rope_tables(pos)
        return _mix_in(h, norm2, mods[3], mods[4], w_in_b, qg, kg, grp, cos_t, sin_a, sin_b, tm=tm,
                       transposed=transposed)

    xs = x_sample[:, 0]
    pos_s = jnp.full((db,), past_len, jnp.int32)
    hs = ffn1(xs, ada_s, db)
    qs, ks, _, vs, _, glu_s, ga_s, gb_s = mix_in(hs, ada_s, pos_s, db, transposed=False)
    qs3, ks3, vs3 = (a.astype(F32).reshape(db, N_HEADS, V_DIM) for a in (qs, ks, vs))

    state = _decode_pages(page_table, qs3, cache_k.reshape(cache_k.shape[1:]), cache_v.reshape(cache_v.shape[1:]), red)

    xp = x_prompt[0]
    pos_p = jnp.arange(seq, dtype=jnp.int32)
    hp = ffn1(xp, ada_p, cfg["ffn_tm"])
    qtp, kp, kpb, vp, vtp, glu_p, ga_p, gb_p = mix_in(hp, ada_p, pos_p, cfg["mix_in_tm"], transposed=True)
    op = _prompt_attention(qtp, kpb, vtp, lam_vecs, subln.reshape(V_DIM, 1), tq=cfg["attn_tq"], tk=cfg["attn_tk"],
                           lam_init=lam_init)
    hp2 = _mix_out_prompt(glu_p, cw, conv_b, conv_ln_g, conv_ln_b, w_co, b_conv_out, op, w_ao, ga_p, gb_p, hp,
                          ada_p[5], w_o, tm=cfg["mix_out_tm"])
    yp = ffn2(hp2, ada_p, cfg["ffn_tm"])

    os_ = _decode_finish(qs3, ks3, vs3, state, lam_vecs, subln, red, lam_init=lam_init)
    padded_s = jnp.concatenate([state_conv[0], glu_s[:, None, :]], axis=1)
    win = jnp.pad(padded_s.transpose(1, 0, 2), ((0, CONV_HALO - CONV_WIDTH), (0, 0), (0, 0)))
    hs2 = _mix_out_sample(win, cw, conv_b, conv_ln_g, conv_ln_b, w_co, b_conv_out,
                          os_.reshape(db, ATTN_WIDTH), w_ao, ga_s, gb_s, hs, ada_s[5], w_o)
    ys = ffn2(hs2, ada_s, db)

    return (yp[None],
            ys[:, None, :],
            kp.reshape(1, 1, seq, N_HEADS, 2 * HEAD_DIM),
            vp.reshape(1, 1, seq, N_HEADS, V_DIM),
            glu_p[seq - CONV_STATE:][None, None],
            ks.reshape(1, db, 1, N_HEADS, 2 * HEAD_DIM),
            vs.reshape(1, db, 1, N_HEADS, V_DIM),
            padded_s[None, :, 1:, :])
```

```python
import functools
import math

import jax
import jax.numpy as jnp
from jax import lax
from jax.experimental import pallas as pl
from jax.experimental.pallas import tpu as pltpu

N_HEADS = 8
HEAD_DIM = 64
V_DIM = 2 * HEAD_DIM
QK_COLS = N_HEADS * 2 * HEAD_DIM
ATTN_WIDTH = N_HEADS * V_DIM
ROT_DIM = HEAD_DIM // 4
ROPE_THETA = 500000.0
PAGE_SIZE = 128
CONV_WIDTH = 31
CONV_STATE = CONV_WIDTH - 1
EPS = 1e-6
NEG_INF = -1e30
N_MOD = 9
QK_SCALE_LOG2 = HEAD_DIM ** -0.5 * math.log2(math.e)
V_ROWS = V_DIM + 16

LANES = 128
SUBLANES = 8
MXU_DIM = 256
VMEM_BYTES_V7X = 64 * 1024 * 1024
VMEM_LIMIT_CAP = VMEM_BYTES_V7X - 6 * 1024 * 1024

CONV_HALO = 32
CONV_ROW_CHUNK = 128
MIX_IN_ROW_CHUNK = 256
ATTN_LOOKAHEAD = 1
DECODE_PAGES_PER_STEP = 16
DECODE_STATE_ROWS = 4
REDUCE_WAYS = 8

F32 = jnp.float32
BF16 = jnp.bfloat16


def _lambda_init(layer_idx):
    return 0.8 - 0.6 * math.exp(-0.3 * layer_idx)


def _vmem_limit(block_bytes):
    return int(min(VMEM_LIMIT_CAP, block_bytes + block_bytes // 4 + (4 << 20)))


def _nbytes(shape, dtype):
    return math.prod(shape) * jnp.dtype(dtype).itemsize


def _dot(a, b):
    return jnp.dot(a, b, preferred_element_type=F32)


def _sigmoid(x):
    return 1.0 / (1.0 + jnp.exp(-x))


def _rmsnorm_mod(x, gain, shift, scale):
    ms = jnp.mean(x * x, axis=-1, keepdims=True)
    n = x * lax.rsqrt(ms + EPS) * gain
    return n * (1.0 + scale) + shift


def _ada_kernel(c_ref, w_ref, b_ref, o_ref):
    c = c_ref[...]
    a = (c * _sigmoid(c)).astype(BF16)
    o_ref[...] = _dot(a, w_ref[...].astype(BF16)) + b_ref[...]


def _ada(c, w, b):
    m, d = c.shape
    n = w.shape[1]
    tn = d // 2
    assert n % tn == 0 and tn % LANES == 0
    blocks = 2 * (_nbytes((m, d), F32) + _nbytes((d, tn), F32) + _nbytes((m, tn), F32))
    return pl.pallas_call(
        _ada_kernel,
        out_shape=jax.ShapeDtypeStruct((m, n), F32),
        grid=(n // tn,),
        in_specs=[pl.BlockSpec((m, d), lambda j: (0, 0)),
                  pl.BlockSpec((d, tn), lambda j: (0, j)),
                  pl.BlockSpec((1, tn), lambda j: (0, j))],
        out_specs=pl.BlockSpec((m, tn), lambda j: (0, j)),
        compiler_params=pltpu.CompilerParams(dimension_semantics=("arbitrary",),
                                             vmem_limit_bytes=_vmem_limit(blocks)),
        name="ada",
    )(c, w, b)


def _ffn_kernel(x_ref, ng_ref, sh_ref, sc_ref, g_ref, wg_ref, wu_ref, wd_ref, o_ref, n_ref, acc_ref, *, d_ff):
    j = pl.program_id(1)
    tf = wg_ref.shape[1]

    @pl.when(j == 0)
    def _():
        n = _rmsnorm_mod(x_ref[...], ng_ref[...], sh_ref[...], sc_ref[...])
        n_ref[...] = n.astype(BF16)
        acc_ref[...] = jnp.zeros_like(acc_ref)

    n = n_ref[...]
    a = _dot(n, wg_ref[...])
    u = _dot(n, wu_ref[...])
    hmid = a * _sigmoid(a) * u
    wd = wd_ref[...]
    if d_ff % tf:
        valid = d_ff - j * tf
        hmid = jnp.where(lax.broadcasted_iota(jnp.int32, hmid.shape, 1) < valid, hmid, 0.0)
        wd = jnp.where(lax.broadcasted_iota(jnp.int32, wd.shape, 0) < valid, wd, jnp.zeros_like(wd))
    acc_ref[...] += _dot(hmid.astype(BF16), wd)

    @pl.when(j == pl.num_programs(1) - 1)
    def _():
        o_ref[...] = x_ref[...] + 0.5 * g_ref[...] * acc_ref[...]


def _mod_spec(mod, tm, d):
    if mod.shape[0] == 1:
        return pl.BlockSpec((1, d), lambda i, j: (0, 0))
    return pl.BlockSpec((tm, d), lambda i, j: (i, 0))


def _ffn(x, norm_g, shift, scale, gate, wg, wu, wd, *, tm, tf):
    m, d = x.shape
    d_ff = wg.shape[1]
    tm = min(tm, m)
    assert m % tm == 0
    blocks = (2 * (2 * _nbytes((tm, d), F32) + 2 * _nbytes((d, tf), BF16) + _nbytes((tf, d), BF16))
              + _nbytes((tm, d), BF16) + _nbytes((tm, d), F32) + 2 * 3 * _nbytes((min(shift.shape[0], tm), d), F32))
    return pl.pallas_call(
        functools.partial(_ffn_kernel, d_ff=d_ff),
        out_shape=jax.ShapeDtypeStruct((m, d), F32),
        grid=(m // tm, pl.cdiv(d_ff, tf)),
        in_specs=[pl.BlockSpec((tm, d), lambda i, j: (i, 0)),
                  pl.BlockSpec((1, d), lambda i, j: (0, 0)),
                  _mod_spec(shift, tm, d), _mod_spec(scale, tm, d), _mod_spec(gate, tm, d),
                  pl.BlockSpec((d, tf), lambda i, j: (0, j)),
                  pl.BlockSpec((d, tf), lambda i, j: (0, j)),
                  pl.BlockSpec((tf, d), lambda i, j: (j, 0))],
        out_specs=pl.BlockSpec((tm, d), lambda i, j: (i, 0)),
        scratch_shapes=[pltpu.VMEM((tm, d), BF16), pltpu.VMEM((tm, d), F32)],
        compiler_params=pltpu.CompilerParams(dimension_semantics=("parallel", "arbitrary"),
                                             vmem_limit_bytes=_vmem_limit(blocks)),
        name="ffn",
    )(x, norm_g, shift, scale, gate, wg, wu, wd)


def _subhead_rms_rope(z, gain, grp, cos_t, sin_a, sin_b):
    tn = z.shape[1]
    zz = (z * z).astype(BF16)
    ms = jnp.concatenate([_dot(zz[:, c:c + MXU_DIM], grp) for c in range(0, tn, MXU_DIM)], axis=1)
    y = z * lax.rsqrt(ms + EPS) * gain
    outs = []
    for c in range(0, tn, LANES):
        yc = y[:, c:c + LANES]
        outs.append(yc * cos_t + pltpu.roll(yc, LANES - ROT_DIM // 2, 1) * sin_a
                    + pltpu.roll(yc, ROT_DIM // 2, 1) * sin_b)
    return jnp.concatenate(outs, axis=1)


def _mix_in_kernel(h_ref, ng_ref, sh_ref, sc_ref, w_ref, qg_ref, kg_ref, grp_ref, cos_ref, sa_ref, sb_ref,
                   q_ref, k_ref, kb_ref, v_ref, vb_ref, glu_ref, ga_ref, gb_ref, u_ref, zc_ref, *, nq, transposed):
    j = pl.program_id(1)

    tm = u_ref.shape[0]
    rc = min(MIX_IN_ROW_CHUNK, tm)

    def column_group(lo, hi, epilogue):
        @pl.when((j >= lo) & (j < hi))
        def _():
            for r0 in range(0, tm, rc):
                rows = slice(r0, r0 + rc)
                epilogue(rows, _dot(u_ref[rows, :], w_ref[...]))

    def rope(z, rows, gain_ref):
        return _subhead_rms_rope(z, gain_ref[...], grp_ref[...], cos_ref[rows, :], sa_ref[rows, :], sb_ref[rows, :])

    def q_epilogue(rows, z):
        r = rope(z, rows, qg_ref) * QK_SCALE_LOG2
        if transposed:
            q_ref[:, rows] = r.T.astype(BF16)
        else:
            q_ref[rows, :] = r.astype(BF16)

    def k_epilogue(rows, z):
        r = rope(z, rows, kg_ref)
        k_ref[rows, :] = r
        kb_ref[rows, :] = r.astype(BF16)

    def v_epilogue(rows, z):
        v_ref[rows, :] = z
        if transposed:
            heads = vb_ref.shape[0]
            vb_ref[:, 0:V_DIM, rows] = z.T.reshape(heads, V_DIM, rc).astype(BF16)
            vb_ref[:, V_DIM:, rows] = jnp.ones((heads, V_ROWS - V_DIM, rc), BF16)
        else:
            vb_ref[rows, :] = z.astype(BF16)

    def zc_epilogue(rows, z):
        zc_ref[rows, :] = z

    def glu_epilogue(rows, z):
        glu_ref[rows, :] = zc_ref[rows, :] * _sigmoid(z)

    def ga_epilogue(rows, z):
        ga_ref[rows, :] = _sigmoid(z)

    def gb_epilogue(rows, z):
        gb_ref[rows, :] = _sigmoid(z)

    @pl.when(j == 0)
    def _():
        u_ref[...] = _rmsnorm_mod(h_ref[...], ng_ref[...], sh_ref[...], sc_ref[...]).astype(BF16)

    column_group(0, nq, q_epilogue)
    column_group(nq, 2 * nq, k_epilogue)
    column_group(2 * nq, 3 * nq, v_epilogue)
    column_group(3 * nq, 3 * nq + 1, zc_epilogue)
    column_group(3 * nq + 1, 3 * nq + 2, glu_epilogue)
    column_group(3 * nq + 2, 3 * nq + 4, ga_epilogue)
    column_group(3 * nq + 4, 3 * nq + 6, gb_epilogue)


def _mix_in(h, norm_g, shift, scale, w_in, q_gain, k_gain, grp, cos_t, sin_a, sin_b, *, tm, transposed):
    m, d = h.shape
    tn = d // 2
    nq = QK_COLS // tn
    assert QK_COLS % tn == 0 and tn % LANES == 0
    ncol = w_in.shape[1] // tn
    assert ncol == 3 * nq + 6
    tm = min(tm, m)
    assert m % tm == 0

    def col(lo):
        return lambda i, j: (i, jnp.clip(j - lo, 0, nq - 1))

    def col2(lo):
        return lambda i, j: (i, jnp.clip(j - lo, 0, 1))

    row = lambda i, j: (i, 0)
    const = lambda i, j: (0, 0)
    blocks = (2 * (_nbytes((tm, d), F32) + _nbytes((d, tn), BF16) + 3 * _nbytes((tm, LANES), F32)
                   + 6 * _nbytes((tm, tn), F32) + 3 * _nbytes((tm, tn), BF16))
              + _nbytes((tm, d), BF16) + _nbytes((tm, tn), F32) + 2 * 2 * _nbytes((min(shift.shape[0], tm), d), F32))
    gsz = grp.shape[0]
    if transposed:
        assert tm % LANES == 0 and tn % V_DIM == 0
        q_shape = jax.ShapeDtypeStruct((QK_COLS, m), BF16)
        q_spec = pl.BlockSpec((tn, tm), lambda i, j: (jnp.clip(j, 0, nq - 1), i))
        v_shape = jax.ShapeDtypeStruct((N_HEADS, V_ROWS, m), BF16)
        v_spec = pl.BlockSpec((tn // V_DIM, V_ROWS, tm), lambda i, j: (jnp.clip(j - 2 * nq, 0, nq - 1), 0, i))
    else:
        q_shape = jax.ShapeDtypeStruct((m, QK_COLS), BF16)
        q_spec = pl.BlockSpec((tm, tn), col(0))
        v_shape = jax.ShapeDtypeStruct((m, ATTN_WIDTH), BF16)
        v_spec = pl.BlockSpec((tm, tn), col(2 * nq))
    outs = pl.pallas_call(
        functools.partial(_mix_in_kernel, nq=nq, transposed=transposed),
        out_shape=[q_shape,
                   jax.ShapeDtypeStruct((m, QK_COLS), F32),
                   jax.ShapeDtypeStruct((m, QK_COLS), BF16),
                   jax.ShapeDtypeStruct((m, ATTN_WIDTH), F32),
                   v_shape,
                   jax.ShapeDtypeStruct((m, tn), F32),
                   jax.ShapeDtypeStruct((m, d), F32),
                   jax.ShapeDtypeStruct((m, d), F32)],
        grid=(m // tm, ncol),
        in_specs=[pl.BlockSpec((tm, d), row),
                  pl.BlockSpec((1, d), const),
                  _mod_spec(shift, tm, d), _mod_spec(scale, tm, d),
                  pl.BlockSpec((d, tn), lambda i, j: (0, j)),
                  pl.BlockSpec((1, tn), const), pl.BlockSpec((1, tn), const),
                  pl.BlockSpec((gsz, gsz), const),
                  pl.BlockSpec((tm, LANES), row), pl.BlockSpec((tm, LANES), row), pl.BlockSpec((tm, LANES), row)],
        out_specs=[q_spec,
                   pl.BlockSpec((tm, tn), col(nq)), pl.BlockSpec((tm, tn), col(nq)),
                   pl.BlockSpec((tm, tn), col(2 * nq)), v_spec,
                   pl.BlockSpec((tm, tn), row),
                   pl.BlockSpec((tm, tn), col2(3 * nq + 2)),
                   pl.BlockSpec((tm, tn), col2(3 * nq + 4))],
        scratch_shapes=[pltpu.VMEM((tm, d), BF16), pltpu.VMEM((tm, tn), F32)],
        compiler_params=pltpu.CompilerParams(dimension_semantics=("parallel", "arbitrary"),
                                             vmem_limit_bytes=_vmem_limit(blocks)),
        name="mix_in",
    )(h, norm_g, shift, scale, w_in, q_gain, k_gain, grp, cos_t, sin_a, sin_b)
    return outs


def _lambda_value(lam_ref, lam_init):
    lv = lam_ref[...]
    l1 = jnp.sum(lv[0:1] * lv[1:2], axis=-1, keepdims=True)
    l2 = jnp.sum(lv[2:3] * lv[3:4], axis=-1, keepdims=True)
    return jnp.exp(l1) - jnp.exp(l2) + lam_init


def _subln(o, gain, lam_init):
    ms = jnp.mean(o * o, axis=-1, keepdims=True)
    return o * lax.rsqrt(ms + EPS) * gain * (1.0 - lam_init)


def _attn_kernel(qi_ref, kj_ref, lam_ref, sg_ref, qt_ref, k_ref, vt_ref, o_ref,
                 qz_ref, m_ref, acc_ref, *, tq, tk, qc, lam_init):
    s_idx = pl.program_id(1)
    qi = qi_ref[s_idx]
    kj = kj_ref[s_idx]
    q_lo = qi * tq
    k_lo = kj * tk
    last_kj = (q_lo + tq - 1) // tk

    @pl.when(kj == 0)
    def _():
        qt = qt_ref[...]
        row = lax.broadcasted_iota(jnp.int32, qt.shape, 0)
        zero = jnp.zeros_like(qt)
        qz_ref[0] = jnp.where(row < HEAD_DIM, qt, zero)
        qz_ref[1] = jnp.where(row >= HEAD_DIM, qt, zero)
        m_ref[...] = jnp.full_like(m_ref, NEG_INF)
        acc_ref[...] = jnp.zeros_like(acc_ref)

    def step(delta):
        k = k_ref[...]
        vt = vt_ref[...]
        chunks = []
        for c0 in range(0, tq, qc):
            if delta is None or c0 >= delta + tk - 1:
                chunks.append((c0, False))
            elif c0 + qc - 1 >= delta:
                chunks.append((c0, True))

        def scores(c0):
            return [_dot(k, qz_ref[sub, :, c0:c0 + qc]) for sub in range(2)]

        ahead = [scores(c0) for c0, _ in chunks[:ATTN_LOOKAHEAD]]
        for idx, (c0, masked) in enumerate(chunks):
            cur = ahead.pop(0)
            if idx + ATTN_LOOKAHEAD < len(chunks):
                ahead.append(scores(chunks[idx + ATTN_LOOKAHEAD][0]))
            cols = slice(c0, c0 + qc)
            if masked:
                q_pos = c0 + lax.broadcasted_iota(jnp.int32, (tk, qc), 1)
                k_pos = delta + lax.broadcasted_iota(jnp.int32, (tk, qc), 0)
                keep = q_pos >= k_pos
            probs, alphas, m_news = [], [], []
            for sub in range(2):
                st = jnp.where(keep, cur[sub], NEG_INF) if masked else cur[sub]
                m_old = m_ref[sub, :, cols]
                m_new = jnp.maximum(m_old, jnp.max(st, axis=0, keepdims=True))
                alphas.append(jnp.exp2(m_old - m_new))
                probs.append(jnp.exp2(st - m_new).astype(BF16))
                m_news.append(m_new)
            for sub in range(2):
                acc_ref[sub, :, cols] = alphas[sub] * acc_ref[sub, :, cols] + _dot(vt, probs[sub])
                m_ref[sub, :, cols] = m_news[sub]

    rel = k_lo - q_lo

    @pl.when(rel < 0)
    def _():
        step(None)

    for delta in range(0, tq, tk):
        @pl.when(rel == delta)
        def _(delta=delta):
            step(delta)

    @pl.when(kj == last_kj)
    def _():
        lam = _lambda_value(lam_ref, lam_init)
        num = [acc_ref[sub, 0:V_DIM, :] for sub in range(2)]
        den = [acc_ref[sub, V_DIM:V_DIM + 1, :] for sub in range(2)]
        ot = num[0] * (1.0 / den[0]) - lam * (num[1] * (1.0 / den[1]))
        ms = jnp.mean(ot * ot, axis=0, keepdims=True)
        ot = ot * lax.rsqrt(ms + EPS) * sg_ref[...] * (1.0 - lam_init)
        o_ref[...] = ot.T.astype(o_ref.dtype)


def _prompt_attention(qt, k, vt, lam_vecs, subln_col, *, tq, tk, lam_init):
    t = k.shape[0]
    tq = min(tq, t)
    tk = min(tk, t)
    qc = min(MXU_DIM, tq)
    assert t % tq == 0 and tq % tk == 0 and tq % qc == 0
    steps = [(a, b) for a in range(t // tq) for b in range((a * tq + tq - 1) // tk + 1)]
    qi_arr = jnp.asarray([a for a, _ in steps], jnp.int32)
    kj_arr = jnp.asarray([b for _, b in steps], jnp.int32)
    blocks = (2 * (2 * _nbytes((tq, V_DIM), BF16) + 2 * _nbytes((tk, V_ROWS), BF16))
              + 2 * _nbytes((tq, V_DIM), BF16) + 2 * _nbytes((SUBLANES, tq), F32) + 2 * _nbytes((tq, V_ROWS), F32)
              + 16 * _nbytes((tk, qc), F32))
    grid_spec = pltpu.PrefetchScalarGridSpec(
        num_scalar_prefetch=2,
        grid=(N_HEADS, len(steps)),
        in_specs=[pl.BlockSpec((4, HEAD_DIM), lambda h, s, qi, kj: (0, 0)),
                  pl.BlockSpec((V_DIM, 1), lambda h, s, qi, kj: (0, 0)),
                  pl.BlockSpec((V_DIM, tq), lambda h, s, qi, kj: (h, qi[s])),
                  pl.BlockSpec((tk, V_DIM), lambda h, s, qi, kj: (kj[s], h)),
                  pl.BlockSpec((None, V_ROWS, tk), lambda h, s, qi, kj: (h, 0, kj[s]))],
        out_specs=pl.BlockSpec((tq, V_DIM), lambda h, s, qi, kj: (qi[s], h)),
        scratch_shapes=[pltpu.VMEM((2, V_DIM, tq), BF16),
                        pltpu.VMEM((2, 1, tq), F32),
                        pltpu.VMEM((2, V_ROWS, tq), F32)])
    return pl.pallas_call(
        functools.partial(_attn_kernel, tq=tq, tk=tk, qc=qc, lam_init=lam_init),
        out_shape=jax.ShapeDtypeStruct((t, ATTN_WIDTH), BF16),
        grid_spec=grid_spec,
        compiler_params=pltpu.CompilerParams(dimension_semantics=("parallel", "arbitrary"),
                                             vmem_limit_bytes=_vmem_limit(blocks)),
        name="prompt_attn",
    )(qi_arr, kj_arr, lam_vecs, subln_col, qt, k, vt)


def _reduce_positions(x, op):
    n = x.shape[0]
    if n > REDUCE_WAYS and n % REDUCE_WAYS == 0:
        x = op(x.reshape(n // REDUCE_WAYS, REDUCE_WAYS, *x.shape[1:]), axis=0)
    return op(x, axis=0)


def _spread_half(x, sub):
    lane = lax.broadcasted_iota(jnp.int32, x.shape, x.ndim - 1)
    own = (lane < HEAD_DIM) if sub == 0 else (lane >= HEAD_DIM)
    return jnp.where(own, x, pltpu.roll(x, HEAD_DIM, x.ndim - 1))


def _page_scores(q, red_ref, k_ref):
    prod = (k_ref[...] * q[None]).reshape(PAGE_SIZE * N_HEADS, V_DIM).astype(BF16)
    return _dot(prod, red_ref[...])


def _page_softmax(scores, v_ref, spread_ref, state):
    m, l, acc0, acc1 = state
    s = scores.reshape(PAGE_SIZE, N_HEADS, LANES)
    m_new = jnp.maximum(m, _reduce_positions(s, jnp.max))
    alpha = jnp.exp2(m - m_new)
    p = jnp.exp2(s - m_new[None])
    l = alpha * l + _reduce_positions(p, jnp.sum)
    p_full = _dot(p.reshape(PAGE_SIZE * N_HEADS, LANES).astype(BF16), spread_ref[...])
    v = v_ref[...]
    p0 = p_full[:, 0:LANES].reshape(PAGE_SIZE, N_HEADS, LANES)
    p1 = p_full[:, LANES:2 * LANES].reshape(PAGE_SIZE, N_HEADS, LANES)
    acc0 = _spread_half(alpha, 0) * acc0 + _reduce_positions(p0 * v, jnp.sum)
    acc1 = _spread_half(alpha, 1) * acc1 + _reduce_positions(p1 * v, jnp.sum)
    return [m_new, l, acc0, acc1]


def _decode_finish_kernel(lam_ref, sg_ref, red_ref, q_ref, kn_ref, vn_ref, st_ref, o_ref, *, lam_init):
    rows = q_ref.shape[0] * N_HEADS
    q = q_ref[...].reshape(rows, V_DIM)
    prod = (kn_ref[...].reshape(rows, V_DIM) * q).astype(BF16)
    s = _dot(prod, red_ref[...])
    vn = vn_ref[...].reshape(rows, V_DIM)
    m_old, l_old = (st_ref[:, r].reshape(rows, LANES) for r in range(2))
    m_new = jnp.maximum(m_old, s)
    alpha = jnp.exp2(m_old - m_new)
    p = jnp.exp2(s - m_new)
    l_fin = alpha * l_old + p
    outs = []
    for sub in range(2):
        acc_old = st_ref[:, 2 + sub].reshape(rows, V_DIM)
        num = _spread_half(alpha, sub) * acc_old + _spread_half(p, sub) * vn
        outs.append(num / _spread_half(l_fin, sub))
    o = outs[0] - _lambda_value(lam_ref, lam_init) * outs[1]
    o_ref[...] = _subln(o, sg_ref[...], lam_init)


def _decode_finish(q, k_new, v_new, state, lam_vecs, subln_g, red, *, lam_init):
    db = q.shape[0]
    full = lambda a: pl.BlockSpec(a.shape, lambda i: (0,) * a.ndim)
    args = (lam_vecs, subln_g, red, q, k_new, v_new, state)
    return pl.pallas_call(
        functools.partial(_decode_finish_kernel, lam_init=lam_init),
        out_shape=jax.ShapeDtypeStruct((db * N_HEADS, V_DIM), F32),
        grid=(1,),
        in_specs=[full(a) for a in args],
        out_specs=pl.BlockSpec((db * N_HEADS, V_DIM), lambda i: (0, 0)),
        name="decode_finish",
    )(*args)


def _decode_pages_kernel(pt_ref, q_ref, red_ref, spread_ref, *rest, pages):
    k_refs, v_refs = rest[:pages], rest[pages:2 * pages]
    st_out_ref, st_ref = rest[2 * pages:]
    p_idx = pl.program_id(1)

    @pl.when(p_idx == 0)
    def _():
        st_ref[0:1] = jnp.full((1, N_HEADS, LANES), NEG_INF, F32)
        st_ref[1:DECODE_STATE_ROWS] = jnp.zeros((DECODE_STATE_ROWS - 1, N_HEADS, LANES), F32)

    q = q_ref[0]
    state = [st_ref[r] for r in range(DECODE_STATE_ROWS)]
    nxt = _page_scores(q, red_ref, k_refs[0])
    for i in range(pages):
        cur = nxt
        if i + 1 < pages:
            nxt = _page_scores(q, red_ref, k_refs[i + 1])
        state = _page_softmax(cur, v_refs[i], spread_ref, state)
    for r in range(DECODE_STATE_ROWS):
        st_ref[r] = state[r]

    @pl.when(p_idx == pl.num_programs(1) - 1)
    def _():
        st_out_ref[0] = st_ref[...]


def _decode_pages(page_table, q, cache_k, cache_v, red, spread):
    db, n_pages = page_table.shape
    pages = math.gcd(DECODE_PAGES_PER_STEP, n_pages)
    page_blk = (None, PAGE_SIZE, N_HEADS, V_DIM)
    page_bytes = _nbytes((PAGE_SIZE, N_HEADS, V_DIM), F32)
    blocks = 2 * 2 * pages * page_bytes + 12 * page_bytes
    state_blk = (1, DECODE_STATE_ROWS, N_HEADS, LANES)

    def page_spec(i):
        return pl.BlockSpec(page_blk, lambda b, p, pt: (pt[b * n_pages + p * pages + i], 0, 0, 0))

    grid_spec = pltpu.PrefetchScalarGridSpec(
        num_scalar_prefetch=1,
        grid=(db, n_pages // pages),
        in_specs=[pl.BlockSpec((1, N_HEADS, V_DIM), lambda b, p, pt: (b, 0, 0)),
                  pl.BlockSpec((V_DIM, LANES), lambda b, p, pt: (0, 0)),
                  pl.BlockSpec((LANES, 2 * LANES), lambda b, p, pt: (0, 0))]
                 + [page_spec(i) for i in range(pages)] * 2,
        out_specs=pl.BlockSpec(state_blk, lambda b, p, pt: (b, 0, 0, 0)),
        scratch_shapes=[pltpu.VMEM(state_blk[1:], F32)])
    return pl.pallas_call(
        functools.partial(_decode_pages_kernel, pages=pages),
        out_shape=jax.ShapeDtypeStruct((db,) + state_blk[1:], F32),
        grid_spec=grid_spec,
        compiler_params=pltpu.CompilerParams(dimension_semantics=("parallel", "arbitrary"),
                                             vmem_limit_bytes=_vmem_limit(blocks)),
        name="decode_pages",
    )(page_table.reshape(-1), q, red, spread, *([cache_k] * pages), *([cache_v] * pages))


def _mix_out_tail(y, cb_ref, lng_ref, lnb_ref, wco_ref, bco_ref, o_ref, wao_ref, ga_ref, gb_ref, h_ref, g_ref,
                  wout_ref, out_ref):
    y = y + cb_ref[...]
    mu = jnp.mean(y, axis=-1, keepdims=True)
    yc = y - mu
    var = jnp.mean(yc * yc, axis=-1, keepdims=True)
    cv = yc * lax.rsqrt(var + EPS) * lng_ref[...] + lnb_ref[...]
    cv = (cv * _sigmoid(cv)).astype(BF16)
    b_out = _dot(cv, wco_ref[...]) + bco_ref[...]
    a_out = _dot(o_ref[...].astype(BF16), wao_ref[...])
    mrg = (ga_ref[...] * a_out + gb_ref[...] * b_out).astype(BF16)
    out_ref[...] = h_ref[...] + g_ref[...] * _dot(mrg, wout_ref[...])


def _mix_out_prompt_kernel(glu_ref, halo_ref, cw_ref, cb_ref, lng_ref, lnb_ref, wco_ref, bco_ref, o_ref, wao_ref,
                           ga_ref, gb_ref, h_ref, g_ref, wout_ref, out_ref, ext_ref, shifted_ref, y_ref, *, tm):
    i = pl.program_id(0)
    c = glu_ref.shape[1]

    @pl.when(i == 0)
    def _():
        ext_ref[0:CONV_HALO, :] = jnp.zeros((CONV_HALO, c), F32)

    @pl.when(i > 0)
    def _():
        ext_ref[0:CONV_HALO, :] = halo_ref[...]

    ext_ref[CONV_HALO:, :] = glu_ref[...]
    rc = min(CONV_ROW_CHUNK, tm)
    off = CONV_HALO - CONV_STATE
    height = tm + CONV_HALO - SUBLANES
    for b in range(1, SUBLANES):
        shifted_ref[b - 1, 0:height, :] = ext_ref[b:b + height, :]
    for r0 in range(0, tm, rc):
        for c0 in range(0, c, LANES):
            acc = jnp.zeros((rc, LANES), F32)
            for j in range(CONV_WIDTH):
                a8, b = (off + j) // SUBLANES * SUBLANES, (off + j) % SUBLANES
                rows, cols = slice(r0 + a8, r0 + a8 + rc), slice(c0, c0 + LANES)
                window = ext_ref[rows, cols] if b == 0 else shifted_ref[b - 1, rows, cols]
                acc = acc + cw_ref[j:j + 1, cols] * window
            y_ref[r0:r0 + rc, c0:c0 + LANES] = acc
    _mix_out_tail(y_ref[...], cb_ref, lng_ref, lnb_ref, wco_ref, bco_ref, o_ref, wao_ref, ga_ref, gb_ref, h_ref,
                  g_ref, wout_ref, out_ref)


def _mix_out_sample_kernel(win_ref, cw_ref, cb_ref, lng_ref, lnb_ref, wco_ref, bco_ref, o_ref, wao_ref,
                           ga_ref, gb_ref, h_ref, g_ref, wout_ref, out_ref):
    y = cw_ref[0:1, :] * win_ref[0]
    for j in range(1, CONV_WIDTH):
        y = y + cw_ref[j:j + 1, :] * win_ref[j]
    _mix_out_tail(y, cb_ref, lng_ref, lnb_ref, wco_ref, bco_ref, o_ref, wao_ref, ga_ref, gb_ref, h_ref, g_ref,
                  wout_ref, out_ref)


def _resident(shape):
    return pl.BlockSpec(shape, lambda i: (0,) * len(shape), pipeline_mode=pl.Buffered(1))


def _mix_out_common_specs(tm, c, d, aw, gate):
    row = lambda i: (i, 0)
    g_spec = (pl.BlockSpec((1, d), lambda i: (0, 0)) if gate.shape[0] == 1 else pl.BlockSpec((tm, d), row))
    return [_resident((CONV_HALO, c)), _resident((1, c)), _resident((1, c)), _resident((1, c)),
            _resident((c, d)), _resident((1, d)),
            pl.BlockSpec((tm, aw), row), _resident((aw, d)),
            pl.BlockSpec((tm, d), row), pl.BlockSpec((tm, d), row), pl.BlockSpec((tm, d), row), g_spec,
            _resident((d, d))]


def _mix_out_bytes(tm, c, d, aw, o_dtype):
    return (2 * (_nbytes((tm, aw), o_dtype) + 4 * _nbytes((tm, d), F32) + _nbytes((tm, d), F32))
            + _nbytes((c, d), BF16) + _nbytes((aw, d), BF16) + _nbytes((d, d), BF16)
            + 8 * _nbytes((tm, d), F32))


def _mix_out_prompt(glu, conv_w, conv_b, ln_g, ln_b, w_co, b_co, o, w_ao, ga, gb, h, gate, w_out, *, tm):
    m, c = glu.shape
    d = h.shape[1]
    aw = o.shape[1]
    tm = min(tm, m)
    assert m % tm == 0 and tm % CONV_HALO == 0
    hb = tm // CONV_HALO
    blocks = _mix_out_bytes(tm, c, d, aw, o.dtype) + (3 + SUBLANES) * _nbytes((tm + CONV_HALO, c), F32)
    return pl.pallas_call(
        functools.partial(_mix_out_prompt_kernel, tm=tm),
        out_shape=jax.ShapeDtypeStruct((m, d), F32),
        grid=(m // tm,),
        in_specs=[pl.BlockSpec((tm, c), lambda i: (i, 0)),
                  pl.BlockSpec((CONV_HALO, c), lambda i: (jnp.maximum(i * hb - 1, 0), 0))]
                 + _mix_out_common_specs(tm, c, d, aw, gate),
        out_specs=pl.BlockSpec((tm, d), lambda i: (i, 0)),
        scratch_shapes=[pltpu.VMEM((tm + CONV_HALO, c), F32), pltpu.VMEM((SUBLANES - 1, tm + CONV_HALO, c), F32),
                        pltpu.VMEM((tm, c), F32)],
        compiler_params=pltpu.CompilerParams(dimension_semantics=("arbitrary",),
                                             vmem_limit_bytes=_vmem_limit(blocks)),
        name="mix_out_prompt",
    )(glu, glu, conv_w, conv_b, ln_g, ln_b, w_co, b_co, o, w_ao, ga, gb, h, gate, w_out)


def _mix_out_sample(win, conv_w, conv_b, ln_g, ln_b, w_co, b_co, o, w_ao, ga, gb, h, gate, w_out):
    _, m, c = win.shape
    d = h.shape[1]
    aw = o.shape[1]
    blocks = _mix_out_bytes(m, c, d, aw, o.dtype) + 2 * _nbytes(win.shape, F32)
    return pl.pallas_call(
        _mix_out_sample_kernel,
        out_shape=jax.ShapeDtypeStruct((m, d), F32),
        grid=(1,),
        in_specs=[pl.BlockSpec(win.shape, lambda i: (0, 0, 0))] + _mix_out_common_specs(m, c, d, aw, gate),
        out_specs=pl.BlockSpec((m, d), lambda i: (0, 0)),
        compiler_params=pltpu.CompilerParams(dimension_semantics=("arbitrary",),
                                             vmem_limit_bytes=_vmem_limit(blocks)),
        name="mix_out_sample",
    )(win, conv_w, conv_b, ln_g, ln_b, w_co, b_co, o, w_ao, ga, gb, h, gate, w_out)


def _rope_tables(pos):
    half = ROT_DIM // 2
    inv = jnp.power(ROPE_THETA, -jnp.arange(half, dtype=F32) * 2.0 / ROT_DIM)
    ang = pos.astype(F32)[:, None] * inv[None, :]
    cos, sin = jnp.cos(ang), jnp.sin(ang)
    t = pos.shape[0]
    pad = jnp.zeros((t, HEAD_DIM - ROT_DIM), F32)
    zero = jnp.zeros((t, half), F32)
    cos_t = jnp.concatenate([cos, cos, pad + 1.0], axis=1)
    sin_a = jnp.concatenate([-sin, zero, pad], axis=1)
    sin_b = jnp.concatenate([zero, sin, pad], axis=1)
    rep = LANES // HEAD_DIM
    return jnp.tile(cos_t, (1, rep)), jnp.tile(sin_a, (1, rep)), jnp.tile(sin_b, (1, rep))


def _group_mean_matrix(n):
    g = jnp.arange(n) // HEAD_DIM
    return jnp.where(g[:, None] == g[None, :], 1.0 / HEAD_DIM, 0.0).astype(BF16)


def _half_sum_matrix():
    half = jnp.arange(V_DIM) // HEAD_DIM
    return (half[:, None] == half[None, :]).astype(BF16)


def _half_spread_matrix():
    lane = jnp.arange(LANES)
    col_half = jnp.arange(2 * LANES) // LANES
    return (lane[:, None] == col_half[None, :] * HEAD_DIM).astype(BF16)


def _tile_config(seq, d_ff):
    return dict(ffn_tm=min(512, seq), ffn_tf=min(512, pl.cdiv(d_ff, LANES) * LANES), mix_in_tm=min(512, seq),
                attn_tq=min(4096, seq), attn_tk=min(512, seq), mix_out_tm=min(256, seq))


def kernel(x_prompt, x_sample, cache_k, cache_v, state_conv, page_table, c_prompt, c_sample, w_ada, b_ada, norm1, w1_gate, w1_up, w1_down, norm2, w_in, q_norm, k_norm, lam_q1, lam_k1, lam_q2, lam_k2, subln, w_attn_out, conv_w, conv_b, conv_ln_g, conv_ln_b, w_conv_out, b_conv_out, w_out, norm3, w2_gate, w2_up, w2_down):
    bsz, seq, d = x_prompt.shape
    db, ts, _ = x_sample.shape
    depth = w_ada.shape[0]
    assert bsz == 1 and ts == 1 and depth == 1, "kernel is specialised to one prompt sequence, one new token, one layer"
    c = conv_w.shape[-1]
    d_ff = w1_gate.shape[-1]
    past_len = page_table.shape[1] * PAGE_SIZE
    cfg = _tile_config(seq, d_ff)
    lam_init = _lambda_init(0)

    tf = cfg["ffn_tf"]
    w1g, w1u, w1d = (w[0].astype(BF16) for w in (w1_gate, w1_up, w1_down))
    w2g, w2u, w2d = (w[0].astype(BF16) for w in (w2_gate, w2_up, w2_down))
    w_in_b = w_in[0].astype(BF16)
    w_ao = w_attn_out[0].astype(BF16)
    w_co = w_conv_out[0].astype(BF16)
    w_o = w_out[0].astype(BF16)

    n_c = bsz + db
    c_all = jnp.concatenate([c_prompt, c_sample, jnp.zeros((-n_c % SUBLANES, d), F32)], axis=0)
    ada = _ada(c_all, w_ada[0], b_ada)
    ada_p = [ada[0:1, i * d:(i + 1) * d] for i in range(N_MOD)]
    ada_s = [ada[1:1 + db, i * d:(i + 1) * d] for i in range(N_MOD)]

    qg = jnp.tile(q_norm[0], c // HEAD_DIM)[None]
    kg = jnp.tile(k_norm[0], c // HEAD_DIM)[None]
    grp = _group_mean_matrix(min(MXU_DIM, c))
    red = _half_sum_matrix()
    lam_vecs = jnp.stack([lam_q1[0], lam_k1[0], lam_q2[0], lam_k2[0]])
    cw = jnp.pad(conv_w[0], ((0, CONV_HALO - CONV_WIDTH), (0, 0)))

    def ffn1(x, mods, tm):
        sh1, sc1, g1 = mods[0:3]
        return _ffn(x, norm1, sh1, sc1, g1, w1g, w1u, w1d, tm=tm, tf=tf)

    def ffn2(h, mods, tm):
        sh3, sc3, g3 = mods[6:9]
        return _ffn(h, norm3, sh3, sc3, g3, w2g, w2u, w2d, tm=tm, tf=tf)

    def mix_in(h, mods, pos, tm, transposed):
        cos_t, sin_a, sin_b = _rope_tables(pos)
        return _mix_in(h, norm2, mods[3], mods[4], w_in_b, qg, kg, grp, cos_t, sin_a, sin_b, tm=tm,
                       transposed=transposed)

    xs = x_sample[:, 0]
    pos_s = jnp.full((db,), past_len, jnp.int32)
    hs = ffn1(xs, ada_s, db)
    qs, ks, _, vs, _, glu_s, ga_s, gb_s = mix_in(hs, ada_s, pos_s, db, transposed=False)
    qs3, ks3, vs3 = (a.astype(F32).reshape(db, N_HEADS, V_DIM) for a in (qs, ks, vs))

    state = _decode_pages(page_table, qs3, cache_k.reshape(cache_k.shape[1:]), cache_v.reshape(cache_v.shape[1:]), red,
                          _half_spread_matrix())

    xp = x_prompt[0]
    pos_p = jnp.arange(seq, dtype=jnp.int32)
    hp = ffn1(xp, ada_p, cfg["ffn_tm"])
    qtp, kp, kpb, vp, vtp, glu_p, ga_p, gb_p = mix_in(hp, ada_p, pos_p, cfg["mix_in_tm"], transposed=True)
    op = _prompt_attention(qtp, kpb, vtp, lam_vecs, subln.reshape(V_DIM, 1), tq=cfg["attn_tq"], tk=cfg["attn_tk"],
                           lam_init=lam_init)
    hp2 = _mix_out_prompt(glu_p, cw, conv_b, conv_ln_g, conv_ln_b, w_co, b_conv_out, op, w_ao, ga_p, gb_p, hp,
                          ada_p[5], w_o, tm=cfg["mix_out_tm"])
    yp = ffn2(hp2, ada_p, cfg["ffn_tm"])

    os_ = _decode_finish(qs3, ks3, vs3, state, lam_vecs, subln, red, lam_init=lam_init)
    padded_s = jnp.concatenate([state_conv[0], glu_s[:, None, :]], axis=1)
    win = jnp.pad(padded_s.transpose(1, 0, 2), ((0, CONV_HALO - CONV_WIDTH), (0, 0), (0, 0)))
    hs2 = _mix_out_sample(win, cw, conv_b, conv_ln_g, conv_ln_b, w_co, b_conv_out,
                          os_.reshape(db, ATTN_WIDTH), w_ao, ga_s, gb_s, hs, ada_s[5], w_o)
    ys = ffn2(hs2, ada_s, db)

    return (yp[None],
            ys[:, None, :],
            kp.reshape(1, 1, seq, N_HEADS, 2 * HEAD_DIM),
            vp.reshape(1, 1, seq, N_HEADS, V_DIM),
            glu_p[seq - CONV_STATE:][None, None],
            ks.reshape(1, db, 1, N_HEADS, 2 * HEAD_DIM),
            vs.reshape(1, db, 1, N_HEADS, V_DIM),
            padded_s[None, :, 1:, :])
```

```python
import functools
import math

import jax
import jax.numpy as jnp
from jax import lax
from jax.experimental import pallas as pl
from jax.experimental.pallas import tpu as pltpu

N_HEADS = 8
HEAD_DIM = 64
V_DIM = 2 * HEAD_DIM
QK_COLS = N_HEADS * 2 * HEAD_DIM
ATTN_WIDTH = N_HEADS * V_DIM
ROT_DIM = HEAD_DIM // 4
ROPE_THETA = 500000.0
PAGE_SIZE = 128
CONV_WIDTH = 31
CONV_STATE = CONV_WIDTH - 1
EPS = 1e-6
NEG_INF = -1e30
N_MOD = 9
QK_SCALE_LOG2 = HEAD_DIM ** -0.5 * math.log2(math.e)
V_ROWS = V_DIM + 16

LANES = 128
SUBLANES = 8
MXU_DIM = 256
VMEM_BYTES_V7X = 64 * 1024 * 1024
VMEM_LIMIT_CAP = VMEM_BYTES_V7X - 6 * 1024 * 1024

CONV_HALO = 32
CONV_ROW_CHUNK = 128
MIX_IN_ROW_CHUNK = 256
DECODE_PAGES_PER_STEP = 16
DECODE_STATE_ROWS = 4
REDUCE_WAYS = 8

F32 = jnp.float32
BF16 = jnp.bfloat16


def _lambda_init(layer_idx):
    return 0.8 - 0.6 * math.exp(-0.3 * layer_idx)


def _vmem_limit(block_bytes):
    return int(min(VMEM_LIMIT_CAP, block_bytes + block_bytes // 4 + (4 << 20)))


def _nbytes(shape, dtype):
    return math.prod(shape) * jnp.dtype(dtype).itemsize


def _dot(a, b):
    return jnp.dot(a, b, preferred_element_type=F32)


def _sigmoid(x):
    return 1.0 / (1.0 + jnp.exp(-x))


def _rmsnorm_mod(x, gain, shift, scale):
    ms = jnp.mean(x * x, axis=-1, keepdims=True)
    n = x * lax.rsqrt(ms + EPS) * gain
    return n * (1.0 + scale) + shift


def _ada_kernel(c_ref, w_ref, b_ref, o_ref):
    c = c_ref[...]
    a = (c * _sigmoid(c)).astype(BF16)
    o_ref[...] = _dot(a, w_ref[...].astype(BF16)) + b_ref[...]


def _ada(c, w, b):
    m, d = c.shape
    n = w.shape[1]
    tn = d // 2
    assert n % tn == 0 and tn % LANES == 0
    blocks = 2 * (_nbytes((m, d), F32) + _nbytes((d, tn), F32) + _nbytes((m, tn), F32))
    return pl.pallas_call(
        _ada_kernel,
        out_shape=jax.ShapeDtypeStruct((m, n), F32),
        grid=(n // tn,),
        in_specs=[pl.BlockSpec((m, d), lambda j: (0, 0)),
                  pl.BlockSpec((d, tn), lambda j: (0, j)),
                  pl.BlockSpec((1, tn), lambda j: (0, j))],
        out_specs=pl.BlockSpec((m, tn), lambda j: (0, j)),
        compiler_params=pltpu.CompilerParams(dimension_semantics=("arbitrary",),
                                             vmem_limit_bytes=_vmem_limit(blocks)),
        name="ada",
    )(c, w, b)


def _ffn_kernel(x_ref, ng_ref, sh_ref, sc_ref, g_ref, wg_ref, wu_ref, wd_ref, o_ref, n_ref, acc_ref, *, d_ff):
    j = pl.program_id(1)
    tf = wg_ref.shape[1]

    @pl.when(j == 0)
    def _():
        n = _rmsnorm_mod(x_ref[...], ng_ref[...], sh_ref[...], sc_ref[...])
        n_ref[...] = n.astype(BF16)
        acc_ref[...] = jnp.zeros_like(acc_ref)

    n = n_ref[...]
    a = _dot(n, wg_ref[...])
    u = _dot(n, wu_ref[...])
    hmid = a * _sigmoid(a) * u
    wd = wd_ref[...]
    if d_ff % tf:
        valid = d_ff - j * tf
        hmid = jnp.where(lax.broadcasted_iota(jnp.int32, hmid.shape, 1) < valid, hmid, 0.0)
        wd = jnp.where(lax.broadcasted_iota(jnp.int32, wd.shape, 0) < valid, wd, jnp.zeros_like(wd))
    acc_ref[...] += _dot(hmid.astype(BF16), wd)

    @pl.when(j == pl.num_programs(1) - 1)
    def _():
        o_ref[...] = x_ref[...] + 0.5 * g_ref[...] * acc_ref[...]


def _mod_spec(mod, tm, d):
    if mod.shape[0] == 1:
        return pl.BlockSpec((1, d), lambda i, j: (0, 0))
    return pl.BlockSpec((tm, d), lambda i, j: (i, 0))


def _ffn(x, norm_g, shift, scale, gate, wg, wu, wd, *, tm, tf):
    m, d = x.shape
    d_ff = wg.shape[1]
    tm = min(tm, m)
    assert m % tm == 0
    blocks = (2 * (2 * _nbytes((tm, d), F32) + 2 * _nbytes((d, tf), BF16) + _nbytes((tf, d), BF16))
              + _nbytes((tm, d), BF16) + _nbytes((tm, d), F32) + 2 * 3 * _nbytes((min(shift.shape[0], tm), d), F32))
    return pl.pallas_call(
        functools.partial(_ffn_kernel, d_ff=d_ff),
        out_shape=jax.ShapeDtypeStruct((m, d), F32),
        grid=(m // tm, pl.cdiv(d_ff, tf)),
        in_specs=[pl.BlockSpec((tm, d), lambda i, j: (i, 0)),
                  pl.BlockSpec((1, d), lambda i, j: (0, 0)),
                  _mod_spec(shift, tm, d), _mod_spec(scale, tm, d), _mod_spec(gate, tm, d),
                  pl.BlockSpec((d, tf), lambda i, j: (0, j)),
                  pl.BlockSpec((d, tf), lambda i, j: (0, j)),
                  pl.BlockSpec((tf, d), lambda i, j: (j, 0))],
        out_specs=pl.BlockSpec((tm, d), lambda i, j: (i, 0)),
        scratch_shapes=[pltpu.VMEM((tm, d), BF16), pltpu.VMEM((tm, d), F32)],
        compiler_params=pltpu.CompilerParams(dimension_semantics=("parallel", "arbitrary"),
                                             vmem_limit_bytes=_vmem_limit(blocks)),
        name="ffn",
    )(x, norm_g, shift, scale, gate, wg, wu, wd)


def _subhead_rms_rope(z, gain, grp, cos_t, sin_a, sin_b):
    tn = z.shape[1]
    zz = (z * z).astype(BF16)
    ms = jnp.concatenate([_dot(zz[:, c:c + MXU_DIM], grp) for c in range(0, tn, MXU_DIM)], axis=1)
    y = z * lax.rsqrt(ms + EPS) * gain
    outs = []
    for c in range(0, tn, LANES):
        yc = y[:, c:c + LANES]
        outs.append(yc * cos_t + pltpu.roll(yc, LANES - ROT_DIM // 2, 1) * sin_a
                    + pltpu.roll(yc, ROT_DIM // 2, 1) * sin_b)
    return jnp.concatenate(outs, axis=1)


def _mix_in_kernel(h_ref, ng_ref, sh_ref, sc_ref, w_ref, qg_ref, kg_ref, grp_ref, cos_ref, sa_ref, sb_ref,
                   q_ref, k_ref, kb_ref, v_ref, vb_ref, glu_ref, ga_ref, gb_ref, u_ref, zc_ref, *, nq, transposed):
    j = pl.program_id(1)

    tm = u_ref.shape[0]
    rc = min(MIX_IN_ROW_CHUNK, tm)

    def column_group(lo, hi, epilogue):
        @pl.when((j >= lo) & (j < hi))
        def _():
            for r0 in range(0, tm, rc):
                rows = slice(r0, r0 + rc)
                epilogue(rows, _dot(u_ref[rows, :], w_ref[...]))

    def rope(z, rows, gain_ref):
        return _subhead_rms_rope(z, gain_ref[...], grp_ref[...], cos_ref[rows, :], sa_ref[rows, :], sb_ref[rows, :])

    def q_epilogue(rows, z):
        r = rope(z, rows, qg_ref) * QK_SCALE_LOG2
        if transposed:
            q_ref[:, rows] = r.T.astype(BF16)
        else:
            q_ref[rows, :] = r.astype(BF16)

    def k_epilogue(rows, z):
        r = rope(z, rows, kg_ref)
        k_ref[rows, :] = r
        kb_ref[rows, :] = r.astype(BF16)

    def v_epilogue(rows, z):
        v_ref[rows, :] = z
        if transposed:
            heads = vb_ref.shape[0]
            vb_ref[:, 0:V_DIM, rows] = z.T.reshape(heads, V_DIM, rc).astype(BF16)
            vb_ref[:, V_DIM:, rows] = jnp.ones((heads, V_ROWS - V_DIM, rc), BF16)
        else:
            vb_ref[rows, :] = z.astype(BF16)

    def zc_epilogue(rows, z):
        zc_ref[rows, :] = z

    def glu_epilogue(rows, z):
        glu_ref[rows, :] = zc_ref[rows, :] * _sigmoid(z)

    def ga_epilogue(rows, z):
        ga_ref[rows, :] = _sigmoid(z)

    def gb_epilogue(rows, z):
        gb_ref[rows, :] = _sigmoid(z)

    @pl.when(j == 0)
    def _():
        u_ref[...] = _rmsnorm_mod(h_ref[...], ng_ref[...], sh_ref[...], sc_ref[...]).astype(BF16)

    column_group(0, nq, q_epilogue)
    column_group(nq, 2 * nq, k_epilogue)
    column_group(2 * nq, 3 * nq, v_epilogue)
    column_group(3 * nq, 3 * nq + 1, zc_epilogue)
    column_group(3 * nq + 1, 3 * nq + 2, glu_epilogue)
    column_group(3 * nq + 2, 3 * nq + 4, ga_epilogue)
    column_group(3 * nq + 4, 3 * nq + 6, gb_epilogue)


def _mix_in(h, norm_g, shift, scale, w_in, q_gain, k_gain, grp, cos_t, sin_a, sin_b, *, tm, transposed):
    m, d = h.shape
    tn = d // 2
    nq = QK_COLS // tn
    assert QK_COLS % tn == 0 and tn % LANES == 0
    ncol = w_in.shape[1] // tn
    assert ncol == 3 * nq + 6
    tm = min(tm, m)
    assert m % tm == 0

    def col(lo):
        return lambda i, j: (i, jnp.clip(j - lo, 0, nq - 1))

    def col2(lo):
        return lambda i, j: (i, jnp.clip(j - lo, 0, 1))

    row = lambda i, j: (i, 0)
    const = lambda i, j: (0, 0)
    blocks = (2 * (_nbytes((tm, d), F32) + _nbytes((d, tn), BF16) + 3 * _nbytes((tm, LANES), F32)
                   + 6 * _nbytes((tm, tn), F32) + 3 * _nbytes((tm, tn), BF16))
              + _nbytes((tm, d), BF16) + _nbytes((tm, tn), F32) + 2 * 2 * _nbytes((min(shift.shape[0], tm), d), F32))
    gsz = grp.shape[0]
    if transposed:
        assert tm % LANES == 0 and tn % V_DIM == 0
        q_shape = jax.ShapeDtypeStruct((QK_COLS, m), BF16)
        q_spec = pl.BlockSpec((tn, tm), lambda i, j: (jnp.clip(j, 0, nq - 1), i))
        v_shape = jax.ShapeDtypeStruct((N_HEADS, V_ROWS, m), BF16)
        v_spec = pl.BlockSpec((tn // V_DIM, V_ROWS, tm), lambda i, j: (jnp.clip(j - 2 * nq, 0, nq - 1), 0, i))
    else:
        q_shape = jax.ShapeDtypeStruct((m, QK_COLS), BF16)
        q_spec = pl.BlockSpec((tm, tn), col(0))
        v_shape = jax.ShapeDtypeStruct((m, ATTN_WIDTH), BF16)
        v_spec = pl.BlockSpec((tm, tn), col(2 * nq))
    outs = pl.pallas_call(
        functools.partial(_mix_in_kernel, nq=nq, transposed=transposed),
        out_shape=[q_shape,
                   jax.ShapeDtypeStruct((m, QK_COLS), F32),
                   jax.ShapeDtypeStruct((m, QK_COLS), BF16),
                   jax.ShapeDtypeStruct((m, ATTN_WIDTH), F32),
                   v_shape,
                   jax.ShapeDtypeStruct((m, tn), F32),
                   jax.ShapeDtypeStruct((m, d), F32),
                   jax.ShapeDtypeStruct((m, d), F32)],
        grid=(m // tm, ncol),
        in_specs=[pl.BlockSpec((tm, d), row),
                  pl.BlockSpec((1, d), const),
                  _mod_spec(shift, tm, d), _mod_spec(scale, tm, d),
                  pl.BlockSpec((d, tn), lambda i, j: (0, j)),
                  pl.BlockSpec((1, tn), const), pl.BlockSpec((1, tn), const),
                  pl.BlockSpec((gsz, gsz), const),
                  pl.BlockSpec((tm, LANES), row), pl.BlockSpec((tm, LANES), row), pl.BlockSpec((tm, LANES), row)],
        out_specs=[q_spec,
                   pl.BlockSpec((tm, tn), col(nq)), pl.BlockSpec((tm, tn), col(nq)),
                   pl.BlockSpec((tm, tn), col(2 * nq)), v_spec,
                   pl.BlockSpec((tm, tn), row),
                   pl.BlockSpec((tm, tn), col2(3 * nq + 2)),
                   pl.BlockSpec((tm, tn), col2(3 * nq + 4))],
        scratch_shapes=[pltpu.VMEM((tm, d), BF16), pltpu.VMEM((tm, tn), F32)],
        compiler_params=pltpu.CompilerParams(dimension_semantics=("parallel", "arbitrary"),
                                             vmem_limit_bytes=_vmem_limit(blocks)),
        name="mix_in",
    )(h, norm_g, shift, scale, w_in, q_gain, k_gain, grp, cos_t, sin_a, sin_b)
    return outs


def _lambda_value(lam_ref, lam_init):
    lv = lam_ref[...]
    l1 = jnp.sum(lv[0:1] * lv[1:2], axis=-1, keepdims=True)
    l2 = jnp.sum(lv[2:3] * lv[3:4], axis=-1, keepdims=True)
    return jnp.exp(l1) - jnp.exp(l2) + lam_init


def _subln(o, gain, lam_init):
    ms = jnp.mean(o * o, axis=-1, keepdims=True)
    return o * lax.rsqrt(ms + EPS) * gain * (1.0 - lam_init)


def _attn_kernel(qi_ref, kj_ref, lam_ref, sg_ref, qt_ref, k_ref, vt_ref, o_ref,
                 qz_ref, m_ref, acc_ref, *, tq, tk, qc, lam_init):
    s_idx = pl.program_id(1)
    qi = qi_ref[s_idx]
    kj = kj_ref[s_idx]
    q_lo = qi * tq
    k_lo = kj * tk
    last_kj = (q_lo + tq - 1) // tk

    @pl.when(kj == 0)
    def _():
        qt = qt_ref[...]
        row = lax.broadcasted_iota(jnp.int32, qt.shape, 0)
        zero = jnp.zeros_like(qt)
        qz_ref[0] = jnp.where(row < HEAD_DIM, qt, zero)
        qz_ref[1] = jnp.where(row >= HEAD_DIM, qt, zero)
        m_ref[...] = jnp.full_like(m_ref, NEG_INF)
        acc_ref[...] = jnp.zeros_like(acc_ref)

    def step(delta):
        k = k_ref[...]
        vt = vt_ref[...]
        chunks = []
        for c0 in range(0, tq, qc):
            if delta is None or c0 >= delta + tk - 1:
                chunks.append((c0, False))
            elif c0 + qc - 1 >= delta:
                chunks.append((c0, True))

        def scores(c0):
            return [_dot(k, qz_ref[sub, :, c0:c0 + qc]) for sub in range(2)]

        nxt = scores(chunks[0][0])
        for idx, (c0, masked) in enumerate(chunks):
            cur = nxt
            if idx + 1 < len(chunks):
                nxt = scores(chunks[idx + 1][0])
            cols = slice(c0, c0 + qc)
            if masked:
                q_pos = c0 + lax.broadcasted_iota(jnp.int32, (tk, qc), 1)
                k_pos = delta + lax.broadcasted_iota(jnp.int32, (tk, qc), 0)
                keep = q_pos >= k_pos
            for sub in range(2):
                st = jnp.where(keep, cur[sub], NEG_INF) if masked else cur[sub]
                m_old = m_ref[sub, :, cols]
                m_new = jnp.maximum(m_old, jnp.max(st, axis=0, keepdims=True))
                alpha = jnp.exp2(m_old - m_new)
                prob = jnp.exp2(st - m_new).astype(BF16)
                acc_ref[sub, :, cols] = alpha * acc_ref[sub, :, cols] + _dot(vt, prob)
                m_ref[sub, :, cols] = m_new

    rel = k_lo - q_lo

    @pl.when(rel < 0)
    def _():
        step(None)

    for delta in range(0, tq, tk):
        @pl.when(rel == delta)
        def _(delta=delta):
            step(delta)

    @pl.when(kj == last_kj)
    def _():
        lam = _lambda_value(lam_ref, lam_init)
        num = [acc_ref[sub, 0:V_DIM, :] for sub in range(2)]
        den = [acc_ref[sub, V_DIM:V_DIM + 1, :] for sub in range(2)]
        ot = num[0] * (1.0 / den[0]) - lam * (num[1] * (1.0 / den[1]))
        ms = jnp.mean(ot * ot, axis=0, keepdims=True)
        ot = ot * lax.rsqrt(ms + EPS) * sg_ref[...] * (1.0 - lam_init)
        o_ref[...] = ot.T.astype(o_ref.dtype)


def _prompt_attention(qt, k, vt, lam_vecs, subln_col, *, tq, tk, lam_init):
    t = k.shape[0]
    tq = min(tq, t)
    tk = min(tk, t)
    qc = min(MXU_DIM, tq)
    assert t % tq == 0 and tq % tk == 0 and tq % qc == 0
    steps = [(a, b) for a in range(t // tq) for b in range((a * tq + tq - 1) // tk + 1)]
    qi_arr = jnp.asarray([a for a, _ in steps], jnp.int32)
    kj_arr = jnp.asarray([b for _, b in steps], jnp.int32)
    blocks = (2 * (2 * _nbytes((tq, V_DIM), BF16) + 2 * _nbytes((tk, V_ROWS), BF16))
              + 2 * _nbytes((tq, V_DIM), BF16) + 2 * _nbytes((SUBLANES, tq), F32) + 2 * _nbytes((tq, V_ROWS), F32)
              + 16 * _nbytes((tk, qc), F32))
    grid_spec = pltpu.PrefetchScalarGridSpec(
        num_scalar_prefetch=2,
        grid=(N_HEADS, len(steps)),
        in_specs=[pl.BlockSpec((4, HEAD_DIM), lambda h, s, qi, kj: (0, 0)),
                  pl.BlockSpec((V_DIM, 1), lambda h, s, qi, kj: (0, 0)),
                  pl.BlockSpec((V_DIM, tq), lambda h, s, qi, kj: (h, qi[s])),
                  pl.BlockSpec((tk, V_DIM), lambda h, s, qi, kj: (kj[s], h)),
                  pl.BlockSpec((None, V_ROWS, tk), lambda h, s, qi, kj: (h, 0, kj[s]))],
        out_specs=pl.BlockSpec((tq, V_DIM), lambda h, s, qi, kj: (qi[s], h)),
        scratch_shapes=[pltpu.VMEM((2, V_DIM, tq), BF16),
                        pltpu.VMEM((2, 1, tq), F32),
                        pltpu.VMEM((2, V_ROWS, tq), F32)])
    return pl.pallas_call(
        functools.partial(_attn_kernel, tq=tq, tk=tk, qc=qc, lam_init=lam_init),
        out_shape=jax.ShapeDtypeStruct((t, ATTN_WIDTH), BF16),
        grid_spec=grid_spec,
        compiler_params=pltpu.CompilerParams(dimension_semantics=("parallel", "arbitrary"),
                                             vmem_limit_bytes=_vmem_limit(blocks)),
        name="prompt_attn",
    )(qi_arr, kj_arr, lam_vecs, subln_col, qt, k, vt)


def _reduce_positions(x, op):
    n = x.shape[0]
    if n > REDUCE_WAYS and n % REDUCE_WAYS == 0:
        x = op(x.reshape(n // REDUCE_WAYS, REDUCE_WAYS, *x.shape[1:]), axis=0)
    return op(x, axis=0)


def _spread_half(x, sub):
    lane = lax.broadcasted_iota(jnp.int32, x.shape, x.ndim - 1)
    own = (lane < HEAD_DIM) if sub == 0 else (lane >= HEAD_DIM)
    return jnp.where(own, x, pltpu.roll(x, HEAD_DIM, x.ndim - 1))


def _page_scores(q, red_ref, k_ref):
    prod = (k_ref[...] * q[None]).reshape(PAGE_SIZE * N_HEADS, V_DIM).astype(BF16)
    return _dot(prod, red_ref[...])


def _page_softmax(scores, v_ref, spread_ref, state):
    m, l, acc0, acc1 = state
    s = scores.reshape(PAGE_SIZE, N_HEADS, LANES)
    m_new = jnp.maximum(m, _reduce_positions(s, jnp.max))
    alpha = jnp.exp2(m - m_new)
    p = jnp.exp2(s - m_new[None])
    l = alpha * l + _reduce_positions(p, jnp.sum)
    p_full = _dot(p.reshape(PAGE_SIZE * N_HEADS, LANES).astype(BF16), spread_ref[...])
    v = v_ref[...]
    p0 = p_full[:, 0:LANES].reshape(PAGE_SIZE, N_HEADS, LANES)
    p1 = p_full[:, LANES:2 * LANES].reshape(PAGE_SIZE, N_HEADS, LANES)
    acc0 = _spread_half(alpha, 0) * acc0 + _reduce_positions(p0 * v, jnp.sum)
    acc1 = _spread_half(alpha, 1) * acc1 + _reduce_positions(p1 * v, jnp.sum)
    return [m_new, l, acc0, acc1]


def _decode_finish_kernel(lam_ref, sg_ref, red_ref, q_ref, kn_ref, vn_ref, st_ref, o_ref, *, lam_init):
    rows = q_ref.shape[0] * N_HEADS
    q = q_ref[...].reshape(rows, V_DIM)
    prod = (kn_ref[...].reshape(rows, V_DIM) * q).astype(BF16)
    s = _dot(prod, red_ref[...])
    vn = vn_ref[...].reshape(rows, V_DIM)
    m_old, l_old = (st_ref[:, r].reshape(rows, LANES) for r in range(2))
    m_new = jnp.maximum(m_old, s)
    alpha = jnp.exp2(m_old - m_new)
    p = jnp.exp2(s - m_new)
    l_fin = alpha * l_old + p
    outs = []
    for sub in range(2):
        acc_old = st_ref[:, 2 + sub].reshape(rows, V_DIM)
        num = _spread_half(alpha, sub) * acc_old + _spread_half(p, sub) * vn
        outs.append(num / _spread_half(l_fin, sub))
    o = outs[0] - _lambda_value(lam_ref, lam_init) * outs[1]
    o_ref[...] = _subln(o, sg_ref[...], lam_init)


def _decode_finish(q, k_new, v_new, state, lam_vecs, subln_g, red, *, lam_init):
    db = q.shape[0]
    full = lambda a: pl.BlockSpec(a.shape, lambda i: (0,) * a.ndim)
    args = (lam_vecs, subln_g, red, q, k_new, v_new, state)
    return pl.pallas_call(
        functools.partial(_decode_finish_kernel, lam_init=lam_init),
        out_shape=jax.ShapeDtypeStruct((db * N_HEADS, V_DIM), F32),
        grid=(1,),
        in_specs=[full(a) for a in args],
        out_specs=pl.BlockSpec((db * N_HEADS, V_DIM), lambda i: (0, 0)),
        name="decode_finish",
    )(*args)


def _decode_pages_kernel(pt_ref, q_ref, red_ref, spread_ref, *rest, pages):
    k_refs, v_refs = rest[:pages], rest[pages:2 * pages]
    st_out_ref, st_ref = rest[2 * pages:]
    p_idx = pl.program_id(1)

    @pl.when(p_idx == 0)
    def _():
        st_ref[0:1] = jnp.full((1, N_HEADS, LANES), NEG_INF, F32)
        st_ref[1:DECODE_STATE_ROWS] = jnp.zeros((DECODE_STATE_ROWS - 1, N_HEADS, LANES), F32)

    q = q_ref[0]
    state = [st_ref[r] for r in range(DECODE_STATE_ROWS)]
    nxt = _page_scores(q, red_ref, k_refs[0])
    for i in range(pages):
        cur = nxt
        if i + 1 < pages:
            nxt = _page_scores(q, red_ref, k_refs[i + 1])
        state = _page_softmax(cur, v_refs[i], spread_ref, state)
    for r in range(DECODE_STATE_ROWS):
        st_ref[r] = state[r]

    @pl.when(p_idx == pl.num_programs(1) - 1)
    def _():
        st_out_ref[0] = st_ref[...]


def _decode_pages(page_table, q, cache_k, cache_v, red, spread):
    db, n_pages = page_table.shape
    pages = math.gcd(DECODE_PAGES_PER_STEP, n_pages)
    page_blk = (None, PAGE_SIZE, N_HEADS, V_DIM)
    page_bytes = _nbytes((PAGE_SIZE, N_HEADS, V_DIM), F32)
    blocks = 2 * 2 * pages * page_bytes + 12 * page_bytes
    state_blk = (1, DECODE_STATE_ROWS, N_HEADS, LANES)

    def page_spec(i):
        return pl.BlockSpec(page_blk, lambda b, p, pt: (pt[b * n_pages + p * pages + i], 0, 0, 0))

    grid_spec = pltpu.PrefetchScalarGridSpec(
        num_scalar_prefetch=1,
        grid=(db, n_pages // pages),
        in_specs=[pl.BlockSpec((1, N_HEADS, V_DIM), lambda b, p, pt: (b, 0, 0)),
                  pl.BlockSpec((V_DIM, LANES), lambda b, p, pt: (0, 0)),
                  pl.BlockSpec((LANES, 2 * LANES), lambda b, p, pt: (0, 0))]
                 + [page_spec(i) for i in range(pages)] * 2,
        out_specs=pl.BlockSpec(state_blk, lambda b, p, pt: (b, 0, 0, 0)),
        scratch_shapes=[pltpu.VMEM(state_blk[1:], F32)])
    return pl.pallas_call(
        functools.partial(_decode_pages_kernel, pages=pages),
        out_shape=jax.ShapeDtypeStruct((db,) + state_blk[1:], F32),
        grid_spec=grid_spec,
        compiler_params=pltpu.CompilerParams(dimension_semantics=("parallel", "arbitrary"),
                                             vmem_limit_bytes=_vmem_limit(blocks)),
        name="decode_pages",
    )(page_table.reshape(-1), q, red, spread, *([cache_k] * pages), *([cache_v] * pages))


def _mix_out_tail(y, cb_ref, lng_ref, lnb_ref, wco_ref, bco_ref, o_ref, wao_ref, ga_ref, gb_ref, h_ref, g_ref,
                  wout_ref, out_ref):
    y = y + cb_ref[...]
    mu = jnp.mean(y, axis=-1, keepdims=True)
    yc = y - mu
    var = jnp.mean(yc * yc, axis=-1, keepdims=True)
    cv = yc * lax.rsqrt(var + EPS) * lng_ref[...] + lnb_ref[...]
    cv = (cv * _sigmoid(cv)).astype(BF16)
    b_out = _dot(cv, wco_ref[...]) + bco_ref[...]
    a_out = _dot(o_ref[...].astype(BF16), wao_ref[...])
    mrg = (ga_ref[...] * a_out + gb_ref[...] * b_out).astype(BF16)
    out_ref[...] = h_ref[...] + g_ref[...] * _dot(mrg, wout_ref[...])


def _mix_out_prompt_kernel(glu_ref, halo_ref, cw_ref, cb_ref, lng_ref, lnb_ref, wco_ref, bco_ref, o_ref, wao_ref,
                           ga_ref, gb_ref, h_ref, g_ref, wout_ref, out_ref, ext_ref, shifted_ref, y_ref, *, tm):
    i = pl.program_id(0)
    c = glu_ref.shape[1]

    @pl.when(i == 0)
    def _():
        ext_ref[0:CONV_HALO, :] = jnp.zeros((CONV_HALO, c), F32)

    @pl.when(i > 0)
    def _():
        ext_ref[0:CONV_HALO, :] = halo_ref[...]

    ext_ref[CONV_HALO:, :] = glu_ref[...]
    rc = min(CONV_ROW_CHUNK, tm)
    off = CONV_HALO - CONV_STATE
    height = tm + CONV_HALO - SUBLANES
    for b in range(1, SUBLANES):
        shifted_ref[b - 1, 0:height, :] = ext_ref[b:b + height, :]
    for r0 in range(0, tm, rc):
        for c0 in range(0, c, LANES):
            acc = jnp.zeros((rc, LANES), F32)
            for j in range(CONV_WIDTH):
                a8, b = (off + j) // SUBLANES * SUBLANES, (off + j) % SUBLANES
                rows, cols = slice(r0 + a8, r0 + a8 + rc), slice(c0, c0 + LANES)
                window = ext_ref[rows, cols] if b == 0 else shifted_ref[b - 1, rows, cols]
                acc = acc + cw_ref[j:j + 1, cols] * window
            y_ref[r0:r0 + rc, c0:c0 + LANES] = acc
    _mix_out_tail(y_ref[...], cb_ref, lng_ref, lnb_ref, wco_ref, bco_ref, o_ref, wao_ref, ga_ref, gb_ref, h_ref,
                  g_ref, wout_ref, out_ref)


def _mix_out_sample_kernel(win_ref, cw_ref, cb_ref, lng_ref, lnb_ref, wco_ref, bco_ref, o_ref, wao_ref,
                           ga_ref, gb_ref, h_ref, g_ref, wout_ref, out_ref):
    y = cw_ref[0:1, :] * win_ref[0]
    for j in range(1, CONV_WIDTH):
        y = y + cw_ref[j:j + 1, :] * win_ref[j]
    _mix_out_tail(y, cb_ref, lng_ref, lnb_ref, wco_ref, bco_ref, o_ref, wao_ref, ga_ref, gb_ref, h_ref, g_ref,
                  wout_ref, out_ref)


def _resident(shape):
    return pl.BlockSpec(shape, lambda i: (0,) * len(shape), pipeline_mode=pl.Buffered(1))


def _mix_out_common_specs(tm, c, d, aw, gate):
    row = lambda i: (i, 0)
    g_spec = (pl.BlockSpec((1, d), lambda i: (0, 0)) if gate.shape[0] == 1 else pl.BlockSpec((tm, d), row))
    return [_resident((CONV_HALO, c)), _resident((1, c)), _resident((1, c)), _resident((1, c)),
            _resident((c, d)), _resident((1, d)),
            pl.BlockSpec((tm, aw), row), _resident((aw, d)),
            pl.BlockSpec((tm, d), row), pl.BlockSpec((tm, d), row), pl.BlockSpec((tm, d), row), g_spec,
            _resident((d, d))]


def _mix_out_bytes(tm, c, d, aw, o_dtype):
    return (2 * (_nbytes((tm, aw), o_dtype) + 4 * _nbytes((tm, d), F32) + _nbytes((tm, d), F32))
            + _nbytes((c, d), BF16) + _nbytes((aw, d), BF16) + _nbytes((d, d), BF16)
            + 8 * _nbytes((tm, d), F32))


def _mix_out_prompt(glu, conv_w, conv_b, ln_g, ln_b, w_co, b_co, o, w_ao, ga, gb, h, gate, w_out, *, tm):
    m, c = glu.shape
    d = h.shape[1]
    aw = o.shape[1]
    tm = min(tm, m)
    assert m % tm == 0 and tm % CONV_HALO == 0
    hb = tm // CONV_HALO
    blocks = _mix_out_bytes(tm, c, d, aw, o.dtype) + (3 + SUBLANES) * _nbytes((tm + CONV_HALO, c), F32)
    return pl.pallas_call(
        functools.partial(_mix_out_prompt_kernel, tm=tm),
        out_shape=jax.ShapeDtypeStruct((m, d), F32),
        grid=(m // tm,),
        in_specs=[pl.BlockSpec((tm, c), lambda i: (i, 0)),
                  pl.BlockSpec((CONV_HALO, c), lambda i: (jnp.maximum(i * hb - 1, 0), 0))]
                 + _mix_out_common_specs(tm, c, d, aw, gate),
        out_specs=pl.BlockSpec((tm, d), lambda i: (i, 0)),
        scratch_shapes=[pltpu.VMEM((tm + CONV_HALO, c), F32), pltpu.VMEM((SUBLANES - 1, tm + CONV_HALO, c), F32),
                        pltpu.VMEM((tm, c), F32)],
        compiler_params=pltpu.CompilerParams(dimension_semantics=("arbitrary",),
                                             vmem_limit_bytes=_vmem_limit(blocks)),
        name="mix_out_prompt",
    )(glu, glu, conv_w, conv_b, ln_g, ln_b, w_co, b_co, o, w_ao, ga, gb, h, gate, w_out)


def _mix_out_sample(win, conv_w, conv_b, ln_g, ln_b, w_co, b_co, o, w_ao, ga, gb, h, gate, w_out):
    _, m, c = win.shape
    d = h.shape[1]
    aw = o.shape[1]
    blocks = _mix_out_bytes(m, c, d, aw, o.dtype) + 2 * _nbytes(win.shape, F32)
    return pl.pallas_call(
        _mix_out_sample_kernel,
        out_shape=jax.ShapeDtypeStruct((m, d), F32),
        grid=(1,),
        in_specs=[pl.BlockSpec(win.shape, lambda i: (0, 0, 0))] + _mix_out_common_specs(m, c, d, aw, gate),
        out_specs=pl.BlockSpec((m, d), lambda i: (0, 0)),
        compiler_params=pltpu.CompilerParams(dimension_semantics=("arbitrary",),
                                             vmem_limit_bytes=_vmem_limit(blocks)),
        name="mix_out_sample",
    )(win, conv_w, conv_b, ln_g, ln_b, w_co, b_co, o, w_ao, ga, gb, h, gate, w_out)


def _rope_tables(pos):
    half = ROT_DIM // 2
    inv = jnp.power(ROPE_THETA, -jnp.arange(half, dtype=F32) * 2.0 / ROT_DIM)
    ang = pos.astype(F32)[:, None] * inv[None, :]
    cos, sin = jnp.cos(ang), jnp.sin(ang)
    t = pos.shape[0]
    pad = jnp.zeros((t, HEAD_DIM - ROT_DIM), F32)
    zero = jnp.zeros((t, half), F32)
    cos_t = jnp.concatenate([cos, cos, pad + 1.0], axis=1)
    sin_a = jnp.concatenate([-sin, zero, pad], axis=1)
    sin_b = jnp.concatenate([zero, sin, pad], axis=1)
    rep = LANES // HEAD_DIM
    return jnp.tile(cos_t, (1, rep)), jnp.tile(sin_a, (1, rep)), jnp.tile(sin_b, (1, rep))


def _group_mean_matrix(n):
    g = jnp.arange(n) // HEAD_DIM
    return jnp.where(g[:, None] == g[None, :], 1.0 / HEAD_DIM, 0.0).astype(BF16)


def _half_sum_matrix():
    half = jnp.arange(V_DIM) // HEAD_DIM
    return (half[:, None] == half[None, :]).astype(BF16)


def _half_spread_matrix():
    lane = jnp.arange(LANES)
    col_half = jnp.arange(2 * LANES) // LANES
    return (lane[:, None] == col_half[None, :] * HEAD_DIM).astype(BF16)


def _tile_config(seq, d_ff):
    return dict(ffn_tm=min(512, seq), ffn_tf=min(512, pl.cdiv(d_ff, LANES) * LANES), mix_in_tm=min(512, seq),
                attn_tq=min(4096, seq), attn_tk=min(512, seq), mix_out_tm=min(256, seq))


def kernel(x_prompt, x_sample, cache_k, cache_v, state_conv, page_table, c_prompt, c_sample, w_ada, b_ada, norm1, w1_gate, w1_up, w1_down, norm2, w_in, q_norm, k_norm, lam_q1, lam_k1, lam_q2, lam_k2, subln, w_attn_out, conv_w, conv_b, conv_ln_g, conv_ln_b, w_conv_out, b_conv_out, w_out, norm3, w2_gate, w2_up, w2_down):
    bsz, seq, d = x_prompt.shape
    db, ts, _ = x_sample.shape
    depth = w_ada.shape[0]
    assert bsz == 1 and ts == 1 and depth == 1, "kernel is specialised to one prompt sequence, one new token, one layer"
    c = conv_w.shape[-1]
    d_ff = w1_gate.shape[-1]
    past_len = page_table.shape[1] * PAGE_SIZE
    cfg = _tile_config(seq, d_ff)
    lam_init = _lambda_init(0)

    tf = cfg["ffn_tf"]
    w1g, w1u, w1d = (w[0].astype(BF16) for w in (w1_gate, w1_up, w1_down))
    w2g, w2u, w2d = (w[0].astype(BF16) for w in (w2_gate, w2_up, w2_down))
    w_in_b = w_in[0].astype(BF16)
    w_ao = w_attn_out[0].astype(BF16)
    w_co = w_conv_out[0].astype(BF16)
    w_o = w_out[0].astype(BF16)

    n_c = bsz + db
    c_all = jnp.concatenate([c_prompt, c_sample, jnp.zeros((-n_c % SUBLANES, d), F32)], axis=0)
    ada = _ada(c_all, w_ada[0], b_ada)
    ada_p = [ada[0:1, i * d:(i + 1) * d] for i in range(N_MOD)]
    ada_s = [ada[1:1 + db, i * d:(i + 1) * d] for i in range(N_MOD)]

    qg = jnp.tile(q_norm[0], c // HEAD_DIM)[None]
    kg = jnp.tile(k_norm[0], c // HEAD_DIM)[None]
    grp = _group_mean_matrix(min(MXU_DIM, c))
    red = _half_sum_matrix()
    lam_vecs = jnp.stack([lam_q1[0], lam_k1[0], lam_q2[0], lam_k2[0]])
    cw = jnp.pad(conv_w[0], ((0, CONV_HALO - CONV_WIDTH), (0, 0)))

    def ffn1(x, mods, tm):
        sh1, sc1, g1 = mods[0:3]
        return _ffn(x, norm1, sh1, sc1, g1, w1g, w1u, w1d, tm=tm, tf=tf)

    def ffn2(h, mods, tm):
        sh3, sc3, g3 = mods[6:9]
        return _ffn(h, norm3, sh3, sc3, g3, w2g, w2u, w2d, tm=tm, tf=tf)

    def mix_in(h, mods, pos, tm, transposed):
        cos_t, sin_a, sin_b = _rope_tables(pos)
        return _mix_in(h, norm2, mods[3], mods[4], w_in_b, qg, kg, grp, cos_t, sin_a, sin_b, tm=tm,
                       transposed=transposed)

    xs = x_sample[:, 0]
    pos_s = jnp.full((db,), past_len, jnp.int32)
    hs = ffn1(xs, ada_s, db)
    qs, ks, _, vs, _, glu_s, ga_s, gb_s = mix_in(hs, ada_s, pos_s, db, transposed=False)
    qs3, ks3, vs3 = (a.astype(F32).reshape(db, N_HEADS, V_DIM) for a in (qs, ks, vs))

    state = _decode_pages(page_table, qs3, cache_k.reshape(cache_k.shape[1:]), cache_v.reshape(cache_v.shape[1:]), red,
                          _half_spread_matrix())

    xp = x_prompt[0]
    pos_p = jnp.arange(seq, dtype=jnp.int32)
    hp = ffn1(xp, ada_p, cfg["ffn_tm"])
    qtp, kp, kpb, vp, vtp, glu_p, ga_p, gb_p = mix_in(hp, ada_p, pos_p, cfg["mix_in_tm"], transposed=True)
    op = _prompt_attention(qtp, kpb, vtp, lam_vecs, subln.reshape(V_DIM, 1), tq=cfg["attn_tq"], tk=cfg["attn_tk"],
                           lam_init=lam_init)
    hp2 = _mix_out_prompt(glu_p, cw, conv_b, conv_ln_g, conv_ln_b, w_co, b_conv_out, op, w_ao, ga_p, gb_p, hp,
                          ada_p[5], w_o, tm=cfg["mix_out_tm"])
    yp = ffn2(hp2, ada_p, cfg["ffn_tm"])

    os_ = _decode_finish(qs3, ks3, vs3, state, lam_vecs, subln, red, lam_init=lam_init)
    padded_s = jnp.concatenate([state_conv[0], glu_s[:, None, :]], axis=1)
    win = jnp.pad(padded_s.transpose(1, 0, 2), ((0, CONV_HALO - CONV_WIDTH), (0, 0), (0, 0)))
    hs2 = _mix_out_sample(win, cw, conv_b, conv_ln_g, conv_ln_b, w_co, b_conv_out,
                          os_.reshape(db, ATTN_WIDTH), w_ao, ga_s, gb_s, hs, ada_s[5], w_o)
    ys = ffn2(hs2, ada_s, db)

    return (yp[None],
            ys[:, None, :],
            kp.reshape(1, 1, seq, N_HEADS, 2 * HEAD_DIM),
            vp.reshape(1, 1, seq, N_HEADS, V_DIM),
            glu_p[seq - CONV_STATE:][None, None],
            ks.reshape(1, db, 1, N_HEADS, 2 * HEAD_DIM),
            vs.reshape(1, db, 1, N_HEADS, V_DIM),
            padded_s[None, :, 1:, :])
```

```python
import functools
import math

import jax
import jax.numpy as jnp
from jax import lax
from jax.experimental import pallas as pl
from jax.experimental.pallas import tpu as pltpu

N_HEADS = 8
HEAD_DIM = 64
V_DIM = 2 * HEAD_DIM
QK_COLS = N_HEADS * 2 * HEAD_DIM
ATTN_WIDTH = N_HEADS * V_DIM
ROT_DIM = HEAD_DIM // 4
ROPE_THETA = 500000.0
PAGE_SIZE = 128
CONV_WIDTH = 31
CONV_STATE = CONV_WIDTH - 1
EPS = 1e-6
NEG_INF = -1e30
N_MOD = 9
QK_SCALE_LOG2 = HEAD_DIM ** -0.5 * math.log2(math.e)
V_ROWS = V_DIM + 16

LANES = 128
SUBLANES = 8
MXU_DIM = 256
VMEM_BYTES_V7X = 64 * 1024 * 1024
VMEM_LIMIT_CAP = VMEM_BYTES_V7X - 6 * 1024 * 1024

CONV_HALO = 32
CONV_ROW_CHUNK = 128
MIX_IN_ROW_CHUNK = 256
FFN_NORM_ROW_CHUNK = 128
DECODE_PAGES_PER_STEP = 16
DECODE_STATE_ROWS = 4
REDUCE_WAYS = 8

F32 = jnp.float32
BF16 = jnp.bfloat16


def _lambda_init(layer_idx):
    return 0.8 - 0.6 * math.exp(-0.3 * layer_idx)


def _vmem_limit(block_bytes):
    return int(min(VMEM_LIMIT_CAP, block_bytes + block_bytes // 4 + (4 << 20)))


def _nbytes(shape, dtype):
    return math.prod(shape) * jnp.dtype(dtype).itemsize


def _dot(a, b):
    return jnp.dot(a, b, preferred_element_type=F32)


def _sigmoid(x):
    return 1.0 / (1.0 + jnp.exp(-x))


def _rmsnorm_mod(x, gain, shift, scale):
    ms = jnp.mean(x * x, axis=-1, keepdims=True)
    n = x * lax.rsqrt(ms + EPS) * gain
    return n * (1.0 + scale) + shift


def _ada_kernel(c_ref, w_ref, b_ref, o_ref):
    c = c_ref[...]
    a = (c * _sigmoid(c)).astype(BF16)
    o_ref[...] = _dot(a, w_ref[...].astype(BF16)) + b_ref[...]


def _ada(c, w, b):
    m, d = c.shape
    n = w.shape[1]
    tn = d // 2
    assert n % tn == 0 and tn % LANES == 0
    blocks = 2 * (_nbytes((m, d), F32) + _nbytes((d, tn), F32) + _nbytes((m, tn), F32))
    return pl.pallas_call(
        _ada_kernel,
        out_shape=jax.ShapeDtypeStruct((m, n), F32),
        grid=(n // tn,),
        in_specs=[pl.BlockSpec((m, d), lambda j: (0, 0)),
                  pl.BlockSpec((d, tn), lambda j: (0, j)),
                  pl.BlockSpec((1, tn), lambda j: (0, j))],
        out_specs=pl.BlockSpec((m, tn), lambda j: (0, j)),
        compiler_params=pltpu.CompilerParams(dimension_semantics=("arbitrary",),
                                             vmem_limit_bytes=_vmem_limit(blocks)),
        name="ada",
    )(c, w, b)


def _ffn_kernel(x_ref, ng_ref, sh_ref, sc_ref, g_ref, wg_ref, wu_ref, wd_ref, o_ref, n_ref, acc_ref, *, d_ff):
    j = pl.program_id(1)
    tm, tf = n_ref.shape[0], wg_ref.shape[1]

    def hidden_tile(n):
        a = _dot(n, wg_ref[...])
        u = _dot(n, wu_ref[...])
        hmid = a * _sigmoid(a) * u
        wd = wd_ref[...]
        if d_ff % tf:
            valid = d_ff - j * tf
            hmid = jnp.where(lax.broadcasted_iota(jnp.int32, hmid.shape, 1) < valid, hmid, 0.0)
            wd = jnp.where(lax.broadcasted_iota(jnp.int32, wd.shape, 0) < valid, wd, jnp.zeros_like(wd))
        return _dot(hmid.astype(BF16), wd)

    @pl.when(j == 0)
    def _():
        rc = min(FFN_NORM_ROW_CHUNK, tm)
        for r0 in range(0, tm, rc):
            rows = slice(r0, r0 + rc)
            mod_rows = lambda ref: ref[rows, :] if ref.shape[0] > 1 else ref[...]
            n = _rmsnorm_mod(x_ref[rows, :], ng_ref[...], mod_rows(sh_ref), mod_rows(sc_ref)).astype(BF16)
            n_ref[rows, :] = n
            acc_ref[rows, :] = hidden_tile(n)

    @pl.when(j > 0)
    def _():
        acc_ref[...] += hidden_tile(n_ref[...])

    @pl.when(j == pl.num_programs(1) - 1)
    def _():
        o_ref[...] = x_ref[...] + 0.5 * g_ref[...] * acc_ref[...]


def _mod_spec(mod, tm, d):
    if mod.shape[0] == 1:
        return pl.BlockSpec((1, d), lambda i, j: (0, 0))
    return pl.BlockSpec((tm, d), lambda i, j: (i, 0))


def _ffn(x, norm_g, shift, scale, gate, wg, wu, wd, *, tm, tf):
    m, d = x.shape
    d_ff = wg.shape[1]
    tm = min(tm, m)
    assert m % tm == 0
    blocks = (2 * (2 * _nbytes((tm, d), F32) + 2 * _nbytes((d, tf), BF16) + _nbytes((tf, d), BF16))
              + _nbytes((tm, d), BF16) + _nbytes((tm, d), F32) + 2 * 3 * _nbytes((min(shift.shape[0], tm), d), F32))
    return pl.pallas_call(
        functools.partial(_ffn_kernel, d_ff=d_ff),
        out_shape=jax.ShapeDtypeStruct((m, d), F32),
        grid=(m // tm, pl.cdiv(d_ff, tf)),
        in_specs=[pl.BlockSpec((tm, d), lambda i, j: (i, 0)),
                  pl.BlockSpec((1, d), lambda i, j: (0, 0)),
                  _mod_spec(shift, tm, d), _mod_spec(scale, tm, d), _mod_spec(gate, tm, d),
                  pl.BlockSpec((d, tf), lambda i, j: (0, j)),
                  pl.BlockSpec((d, tf), lambda i, j: (0, j)),
                  pl.BlockSpec((tf, d), lambda i, j: (j, 0))],
        out_specs=pl.BlockSpec((tm, d), lambda i, j: (i, 0)),
        scratch_shapes=[pltpu.VMEM((tm, d), BF16), pltpu.VMEM((tm, d), F32)],
        compiler_params=pltpu.CompilerParams(dimension_semantics=("parallel", "arbitrary"),
                                             vmem_limit_bytes=_vmem_limit(blocks)),
        name="ffn",
    )(x, norm_g, shift, scale, gate, wg, wu, wd)


def _subhead_rms_rope(z, gain, grp, cos_t, sin_a, sin_b):
    tn = z.shape[1]
    zz = (z * z).astype(BF16)
    ms = jnp.concatenate([_dot(zz[:, c:c + MXU_DIM], grp) for c in range(0, tn, MXU_DIM)], axis=1)
    y = z * lax.rsqrt(ms + EPS) * gain
    outs = []
    for c in range(0, tn, LANES):
        yc = y[:, c:c + LANES]
        outs.append(yc * cos_t + pltpu.roll(yc, LANES - ROT_DIM // 2, 1) * sin_a
                    + pltpu.roll(yc, ROT_DIM // 2, 1) * sin_b)
    return jnp.concatenate(outs, axis=1)


def _mix_in_kernel(h_ref, ng_ref, sh_ref, sc_ref, w_ref, qg_ref, kg_ref, grp_ref, cos_ref, sa_ref, sb_ref,
                   q_ref, k_ref, kb_ref, v_ref, vb_ref, glu_ref, ga_ref, gb_ref, u_ref, zc_ref, *, nq, transposed):
    j = pl.program_id(1)

    tm = u_ref.shape[0]
    rc = min(MIX_IN_ROW_CHUNK, tm)

    def column_group(lo, hi, epilogue):
        @pl.when((j >= lo) & (j < hi))
        def _():
            for r0 in range(0, tm, rc):
                rows = slice(r0, r0 + rc)
                epilogue(rows, _dot(u_ref[rows, :], w_ref[...]))

    def rope(z, rows, gain_ref):
        return _subhead_rms_rope(z, gain_ref[...], grp_ref[...], cos_ref[rows, :], sa_ref[rows, :], sb_ref[rows, :])

    def q_epilogue(rows, z):
        r = rope(z, rows, qg_ref) * QK_SCALE_LOG2
        if transposed:
            q_ref[:, rows] = r.T.astype(BF16)
        else:
            q_ref[rows, :] = r.astype(BF16)

    def k_epilogue(rows, z):
        r = rope(z, rows, kg_ref)
        k_ref[rows, :] = r
        kb_ref[rows, :] = r.astype(BF16)

    def v_epilogue(rows, z):
        v_ref[rows, :] = z
        if transposed:
            heads = vb_ref.shape[0]
            vb_ref[:, 0:V_DIM, rows] = z.T.reshape(heads, V_DIM, rc).astype(BF16)
            vb_ref[:, V_DIM:, rows] = jnp.ones((heads, V_ROWS - V_DIM, rc), BF16)
        else:
            vb_ref[rows, :] = z.astype(BF16)

    def zc_epilogue(rows, z):
        zc_ref[rows, :] = z

    def glu_epilogue(rows, z):
        glu_ref[rows, :] = zc_ref[rows, :] * _sigmoid(z)

    def ga_epilogue(rows, z):
        ga_ref[rows, :] = _sigmoid(z)

    def gb_epilogue(rows, z):
        gb_ref[rows, :] = _sigmoid(z)

    @pl.when(j == 0)
    def _():
        for r0 in range(0, tm, rc):
            rows = slice(r0, r0 + rc)
            mod_rows = lambda ref: ref[rows, :] if ref.shape[0] > 1 else ref[...]
            u = _rmsnorm_mod(h_ref[rows, :], ng_ref[...], mod_rows(sh_ref), mod_rows(sc_ref)).astype(BF16)
            u_ref[rows, :] = u
            q_epilogue(rows, _dot(u, w_ref[...]))

    if nq > 1:
        column_group(1, nq, q_epilogue)
    column_group(nq, 2 * nq, k_epilogue)
    column_group(2 * nq, 3 * nq, v_epilogue)
    column_group(3 * nq, 3 * nq + 1, zc_epilogue)
    column_group(3 * nq + 1, 3 * nq + 2, glu_epilogue)
    column_group(3 * nq + 2, 3 * nq + 4, ga_epilogue)
    column_group(3 * nq + 4, 3 * nq + 6, gb_epilogue)


def _mix_in(h, norm_g, shift, scale, w_in, q_gain, k_gain, grp, cos_t, sin_a, sin_b, *, tm, transposed):
    m, d = h.shape
    tn = d // 2
    nq = QK_COLS // tn
    assert QK_COLS % tn == 0 and tn % LANES == 0
    ncol = w_in.shape[1] // tn
    assert ncol == 3 * nq + 6
    tm = min(tm, m)
    assert m % tm == 0

    def col(lo):
        return lambda i, j: (i, jnp.clip(j - lo, 0, nq - 1))

    def col2(lo):
        return lambda i, j: (i, jnp.clip(j - lo, 0, 1))

    row = lambda i, j: (i, 0)
    const = lambda i, j: (0, 0)
    blocks = (2 * (_nbytes((tm, d), F32) + _nbytes((d, tn), BF16) + 3 * _nbytes((tm, LANES), F32)
                   + 6 * _nbytes((tm, tn), F32) + 3 * _nbytes((tm, tn), BF16))
              + _nbytes((tm, d), BF16) + _nbytes((tm, tn), F32) + 2 * 2 * _nbytes((min(shift.shape[0], tm), d), F32))
    gsz = grp.shape[0]
    if transposed:
        assert tm % LANES == 0 and tn % V_DIM == 0
        q_shape = jax.ShapeDtypeStruct((QK_COLS, m), BF16)
        q_spec = pl.BlockSpec((tn, tm), lambda i, j: (jnp.clip(j, 0, nq - 1), i))
        v_shape = jax.ShapeDtypeStruct((N_HEADS, V_ROWS, m), BF16)
        v_spec = pl.BlockSpec((tn // V_DIM, V_ROWS, tm), lambda i, j: (jnp.clip(j - 2 * nq, 0, nq - 1), 0, i))
    else:
        q_shape = jax.ShapeDtypeStruct((m, QK_COLS), BF16)
        q_spec = pl.BlockSpec((tm, tn), col(0))
        v_shape = jax.ShapeDtypeStruct((m, ATTN_WIDTH), BF16)
        v_spec = pl.BlockSpec((tm, tn), col(2 * nq))
    outs = pl.pallas_call(
        functools.partial(_mix_in_kernel, nq=nq, transposed=transposed),
        out_shape=[q_shape,
                   jax.ShapeDtypeStruct((m, QK_COLS), F32),
                   jax.ShapeDtypeStruct((m, QK_COLS), BF16),
                   jax.ShapeDtypeStruct((m, ATTN_WIDTH), F32),
                   v_shape,
                   jax.ShapeDtypeStruct((m, tn), F32),
                   jax.ShapeDtypeStruct((m, d), F32),
                   jax.ShapeDtypeStruct((m, d), F32)],
        grid=(m // tm, ncol),
        in_specs=[pl.BlockSpec((tm, d), row),
                  pl.BlockSpec((1, d), const),
                  _mod_spec(shift, tm, d), _mod_spec(scale, tm, d),
                  pl.BlockSpec((d, tn), lambda i, j: (0, j)),
                  pl.BlockSpec((1, tn), const), pl.BlockSpec((1, tn), const),
                  pl.BlockSpec((gsz, gsz), const),
                  pl.BlockSpec((tm, LANES), row), pl.BlockSpec((tm, LANES), row), pl.BlockSpec((tm, LANES), row)],
        out_specs=[q_spec,
                   pl.BlockSpec((tm, tn), col(nq)), pl.BlockSpec((tm, tn), col(nq)),
                   pl.BlockSpec((tm, tn), col(2 * nq)), v_spec,
                   pl.BlockSpec((tm, tn), row),
                   pl.BlockSpec((tm, tn), col2(3 * nq + 2)),
                   pl.BlockSpec((tm, tn), col2(3 * nq + 4))],
        scratch_shapes=[pltpu.VMEM((tm, d), BF16), pltpu.VMEM((tm, tn), F32)],
        compiler_params=pltpu.CompilerParams(dimension_semantics=("parallel", "arbitrary"),
                                             vmem_limit_bytes=_vmem_limit(blocks)),
        name="mix_in",
    )(h, norm_g, shift, scale, w_in, q_gain, k_gain, grp, cos_t, sin_a, sin_b)
    return outs


def _lambda_value(lam_ref, lam_init):
    lv = lam_ref[...]
    l1 = jnp.sum(lv[0:1] * lv[1:2], axis=-1, keepdims=True)
    l2 = jnp.sum(lv[2:3] * lv[3:4], axis=-1, keepdims=True)
    return jnp.exp(l1) - jnp.exp(l2) + lam_init


def _subln(o, gain, lam_init):
    ms = jnp.mean(o * o, axis=-1, keepdims=True)
    return o * lax.rsqrt(ms + EPS) * gain * (1.0 - lam_init)


def _attn_kernel(qi_ref, kj_ref, lam_ref, sg_ref, qt_ref, k_ref, vt_ref, o_ref,
                 qz_ref, m_ref, acc_ref, *, tq, tk, qc, lam_init):
    s_idx = pl.program_id(1)
    qi = qi_ref[s_idx]
    kj = kj_ref[s_idx]
    q_lo = qi * tq
    k_lo = kj * tk
    last_kj = (q_lo + tq - 1) // tk

    @pl.when(kj == 0)
    def _():
        qt = qt_ref[...]
        row = lax.broadcasted_iota(jnp.int32, qt.shape, 0)
        zero = jnp.zeros_like(qt)
        qz_ref[0] = jnp.where(row < HEAD_DIM, qt, zero)
        qz_ref[1] = jnp.where(row >= HEAD_DIM, qt, zero)
        m_ref[...] = jnp.full_like(m_ref, NEG_INF)
        acc_ref[...] = jnp.zeros_like(acc_ref)

    def step(delta):
        k = k_ref[...]
        vt = vt_ref[...]
        chunks = []
        for c0 in range(0, tq, qc):
            if delta is None or c0 >= delta + tk - 1:
                chunks.append((c0, False))
            elif c0 + qc - 1 >= delta:
                chunks.append((c0, True))

        def scores(c0):
            return [_dot(k, qz_ref[sub, :, c0:c0 + qc]) for sub in range(2)]

        nxt = scores(chunks[0][0])
        for idx, (c0, masked) in enumerate(chunks):
            cur = nxt
            if idx + 1 < len(chunks):
                nxt = scores(chunks[idx + 1][0])
            cols = slice(c0, c0 + qc)
            if masked:
                q_pos = c0 + lax.broadcasted_iota(jnp.int32, (tk, qc), 1)
                k_pos = delta + lax.broadcasted_iota(jnp.int32, (tk, qc), 0)
                keep = q_pos >= k_pos
            for sub in range(2):
                st = jnp.where(keep, cur[sub], NEG_INF) if masked else cur[sub]
                m_old = m_ref[sub, :, cols]
                m_new = jnp.maximum(m_old, jnp.max(st, axis=0, keepdims=True))
                alpha = jnp.exp2(m_old - m_new)
                prob = jnp.exp2(st - m_new).astype(BF16)
                acc_ref[sub, :, cols] = alpha * acc_ref[sub, :, cols] + _dot(vt, prob)
                m_ref[sub, :, cols] = m_new

    rel = k_lo - q_lo

    @pl.when(rel < 0)
    def _():
        step(None)

    for delta in range(0, tq, tk):
        @pl.when(rel == delta)
        def _(delta=delta):
            step(delta)

    @pl.when(kj == last_kj)
    def _():
        lam = _lambda_value(lam_ref, lam_init)
        num = [acc_ref[sub, 0:V_DIM, :] for sub in range(2)]
        den = [acc_ref[sub, V_DIM:V_DIM + 1, :] for sub in range(2)]
        ot = num[0] * (1.0 / den[0]) - lam * (num[1] * (1.0 / den[1]))
        ms = jnp.mean(ot * ot, axis=0, keepdims=True)
        ot = ot * lax.rsqrt(ms + EPS) * sg_ref[...] * (1.0 - lam_init)
        o_ref[...] = ot.T.astype(o_ref.dtype)


def _prompt_attention(qt, k, vt, lam_vecs, subln_col, *, tq, tk, lam_init):
    t = k.shape[0]
    tq = min(tq, t)
    tk = min(tk, t)
    qc = min(MXU_DIM, tq)
    assert t % tq == 0 and tq % tk == 0 and tq % qc == 0
    steps = [(a, b) for a in range(t // tq) for b in range((a * tq + tq - 1) // tk + 1)]
    qi_arr = jnp.asarray([a for a, _ in steps], jnp.int32)
    kj_arr = jnp.asarray([b for _, b in steps], jnp.int32)
    blocks = (2 * (2 * _nbytes((tq, V_DIM), BF16) + 2 * _nbytes((tk, V_ROWS), BF16))
              + 2 * _nbytes((tq, V_DIM), BF16) + 2 * _nbytes((SUBLANES, tq), F32) + 2 * _nbytes((tq, V_ROWS), F32)
              + 16 * _nbytes((tk, qc), F32))
    grid_spec = pltpu.PrefetchScalarGridSpec(
        num_scalar_prefetch=2,
        grid=(N_HEADS, len(steps)),
        in_specs=[pl.BlockSpec((4, HEAD_DIM), lambda h, s, qi, kj: (0, 0)),
                  pl.BlockSpec((V_DIM, 1), lambda h, s, qi, kj: (0, 0)),
                  pl.BlockSpec((V_DIM, tq), lambda h, s, qi, kj: (h, qi[s])),
                  pl.BlockSpec((tk, V_DIM), lambda h, s, qi, kj: (kj[s], h)),
                  pl.BlockSpec((None, V_ROWS, tk), lambda h, s, qi, kj: (h, 0, kj[s]))],
        out_specs=pl.BlockSpec((tq, V_DIM), lambda h, s, qi, kj: (qi[s], h)),
        scratch_shapes=[pltpu.VMEM((2, V_DIM, tq), BF16),
                        pltpu.VMEM((2, 1, tq), F32),
                        pltpu.VMEM((2, V_ROWS, tq), F32)])
    return pl.pallas_call(
        functools.partial(_attn_kernel, tq=tq, tk=tk, qc=qc, lam_init=lam_init),
        out_shape=jax.ShapeDtypeStruct((t, ATTN_WIDTH), BF16),
        grid_spec=grid_spec,
        compiler_params=pltpu.CompilerParams(dimension_semantics=("parallel", "arbitrary"),
                                             vmem_limit_bytes=_vmem_limit(blocks)),
        name="prompt_attn",
    )(qi_arr, kj_arr, lam_vecs, subln_col, qt, k, vt)


def _reduce_positions(x, op):
    n = x.shape[0]
    if n > REDUCE_WAYS and n % REDUCE_WAYS == 0:
        x = op(x.reshape(n // REDUCE_WAYS, REDUCE_WAYS, *x.shape[1:]), axis=0)
    return op(x, axis=0)


def _spread_half(x, sub):
    lane = lax.broadcasted_iota(jnp.int32, x.shape, x.ndim - 1)
    own = (lane < HEAD_DIM) if sub == 0 else (lane >= HEAD_DIM)
    return jnp.where(own, x, pltpu.roll(x, HEAD_DIM, x.ndim - 1))


def _page_scores(q, red_ref, k_ref):
    prod = (k_ref[...] * q[None]).reshape(PAGE_SIZE * N_HEADS, V_DIM).astype(BF16)
    return _dot(prod, red_ref[...])


def _page_softmax(scores, v_ref, spread_ref, state):
    m, l, acc0, acc1 = state
    s = scores.reshape(PAGE_SIZE, N_HEADS, LANES)
    m_new = jnp.maximum(m, _reduce_positions(s, jnp.max))
    alpha = jnp.exp2(m - m_new)
    p = jnp.exp2(s - m_new[None])
    l = alpha * l + _reduce_positions(p, jnp.sum)
    p_full = _dot(p.reshape(PAGE_SIZE * N_HEADS, LANES).astype(BF16), spread_ref[...])
    v = v_ref[...]
    p0 = p_full[:, 0:LANES].reshape(PAGE_SIZE, N_HEADS, LANES)
    p1 = p_full[:, LANES:2 * LANES].reshape(PAGE_SIZE, N_HEADS, LANES)
    acc0 = _spread_half(alpha, 0) * acc0 + _reduce_positions(p0 * v, jnp.sum)
    acc1 = _spread_half(alpha, 1) * acc1 + _reduce_positions(p1 * v, jnp.sum)
    return [m_new, l, acc0, acc1]


def _decode_finish_kernel(lam_ref, sg_ref, red_ref, q_ref, kn_ref, vn_ref, st_ref, o_ref, *, lam_init):
    rows = q_ref.shape[0] * N_HEADS
    q = q_ref[...].reshape(rows, V_DIM)
    prod = (kn_ref[...].reshape(rows, V_DIM) * q).astype(BF16)
    s = _dot(prod, red_ref[...])
    vn = vn_ref[...].reshape(rows, V_DIM)
    m_old, l_old = (st_ref[:, r].reshape(rows, LANES) for r in range(2))
    m_new = jnp.maximum(m_old, s)
    alpha = jnp.exp2(m_old - m_new)
    p = jnp.exp2(s - m_new)
    l_fin = alpha * l_old + p
    outs = []
    for sub in range(2):
        acc_old = st_ref[:, 2 + sub].reshape(rows, V_DIM)
        num = _spread_half(alpha, sub) * acc_old + _spread_half(p, sub) * vn
        outs.append(num / _spread_half(l_fin, sub))
    o = outs[0] - _lambda_value(lam_ref, lam_init) * outs[1]
    o_ref[...] = _subln(o, sg_ref[...], lam_init)


def _decode_finish(q, k_new, v_new, state, lam_vecs, subln_g, red, *, lam_init):
    db = q.shape[0]
    full = lambda a: pl.BlockSpec(a.shape, lambda i: (0,) * a.ndim)
    args = (lam_vecs, subln_g, red, q, k_new, v_new, state)
    return pl.pallas_call(
        functools.partial(_decode_finish_kernel, lam_init=lam_init),
        out_shape=jax.ShapeDtypeStruct((db * N_HEADS, V_DIM), F32),
        grid=(1,),
        in_specs=[full(a) for a in args],
        out_specs=pl.BlockSpec((db * N_HEADS, V_DIM), lambda i: (0, 0)),
        name="decode_finish",
    )(*args)


def _decode_pages_kernel(pt_ref, q_ref, red_ref, spread_ref, *rest, pages):
    k_refs, v_refs = rest[:pages], rest[pages:2 * pages]
    st_out_ref, st_ref = rest[2 * pages:]
    p_idx = pl.program_id(1)

    @pl.when(p_idx == 0)
    def _():
        st_ref[0:1] = jnp.full((1, N_HEADS, LANES), NEG_INF, F32)
        st_ref[1:DECODE_STATE_ROWS] = jnp.zeros((DECODE_STATE_ROWS - 1, N_HEADS, LANES), F32)

    q = q_ref[0]
    state = [st_ref[r] for r in range(DECODE_STATE_ROWS)]
    nxt = _page_scores(q, red_ref, k_refs[0])
    for i in range(pages):
        cur = nxt
        if i + 1 < pages:
            nxt = _page_scores(q, red_ref, k_refs[i + 1])
        state = _page_softmax(cur, v_refs[i], spread_ref, state)
    for r in range(DECODE_STATE_ROWS):
        st_ref[r] = state[r]

    @pl.when(p_idx == pl.num_programs(1) - 1)
    def _():
        st_out_ref[0] = st_ref[...]


def _decode_pages(page_table, q, cache_k, cache_v, red, spread):
    db, n_pages = page_table.shape
    pages = math.gcd(DECODE_PAGES_PER_STEP, n_pages)
    page_blk = (None, PAGE_SIZE, N_HEADS, V_DIM)
    page_bytes = _nbytes((PAGE_SIZE, N_HEADS, V_DIM), F32)
    blocks = 2 * 2 * pages * page_bytes + 12 * page_bytes
    state_blk = (1, DECODE_STATE_ROWS, N_HEADS, LANES)

    def page_spec(i):
        return pl.BlockSpec(page_blk, lambda b, p, pt: (pt[b * n_pages + p * pages + i], 0, 0, 0))

    grid_spec = pltpu.PrefetchScalarGridSpec(
        num_scalar_prefetch=1,
        grid=(db, n_pages // pages),
        in_specs=[pl.BlockSpec((1, N_HEADS, V_DIM), lambda b, p, pt: (b, 0, 0)),
                  pl.BlockSpec((V_DIM, LANES), lambda b, p, pt: (0, 0)),
                  pl.BlockSpec((LANES, 2 * LANES), lambda b, p, pt: (0, 0))]
                 + [page_spec(i) for i in range(pages)] * 2,
        out_specs=pl.BlockSpec(state_blk, lambda b, p, pt: (b, 0, 0, 0)),
        scratch_shapes=[pltpu.VMEM(state_blk[1:], F32)])
    return pl.pallas_call(
        functools.partial(_decode_pages_kernel, pages=pages),
        out_shape=jax.ShapeDtypeStruct((db,) + state_blk[1:], F32),
        grid_spec=grid_spec,
        compiler_params=pltpu.CompilerParams(dimension_semantics=("parallel", "arbitrary"),
                                             vmem_limit_bytes=_vmem_limit(blocks)),
        name="decode_pages",
    )(page_table.reshape(-1), q, red, spread, *([cache_k] * pages), *([cache_v] * pages))


def _mix_out_tail(y, cb_ref, lng_ref, lnb_ref, wco_ref, bco_ref, o_ref, wao_ref, ga_ref, gb_ref, h_ref, g_ref,
                  wout_ref, out_ref):
    y = y + cb_ref[...]
    mu = jnp.mean(y, axis=-1, keepdims=True)
    yc = y - mu
    var = jnp.mean(yc * yc, axis=-1, keepdims=True)
    cv = yc * lax.rsqrt(var + EPS) * lng_ref[...] + lnb_ref[...]
    cv = (cv * _sigmoid(cv)).astype(BF16)
    b_out = _dot(cv, wco_ref[...]) + bco_ref[...]
    a_out = _dot(o_ref[...].astype(BF16), wao_ref[...])
    mrg = (ga_ref[...] * a_out + gb_ref[...] * b_out).astype(BF16)
    out_ref[...] = h_ref[...] + g_ref[...] * _dot(mrg, wout_ref[...])


def _mix_out_prompt_kernel(glu_ref, halo_ref, cw_ref, cb_ref, lng_ref, lnb_ref, wco_ref, bco_ref, o_ref, wao_ref,
                           ga_ref, gb_ref, h_ref, g_ref, wout_ref, out_ref, ext_ref, shifted_ref, y_ref, *, tm):
    i = pl.program_id(0)
    c = glu_ref.shape[1]

    @pl.when(i == 0)
    def _():
        ext_ref[0:CONV_HALO, :] = jnp.zeros((CONV_HALO, c), F32)

    @pl.when(i > 0)
    def _():
        ext_ref[0:CONV_HALO, :] = halo_ref[...]

    ext_ref[CONV_HALO:, :] = glu_ref[...]
    rc = min(CONV_ROW_CHUNK, tm)
    off = CONV_HALO - CONV_STATE
    height = tm + CONV_HALO - SUBLANES
    for b in range(1, SUBLANES):
        shifted_ref[b - 1, 0:height, :] = ext_ref[b:b + height, :]
    for r0 in range(0, tm, rc):
        for c0 in range(0, c, LANES):
            acc = jnp.zeros((rc, LANES), F32)
            for j in range(CONV_WIDTH):
                a8, b = (off + j) // SUBLANES * SUBLANES, (off + j) % SUBLANES
                rows, cols = slice(r0 + a8, r0 + a8 + rc), slice(c0, c0 + LANES)
                window = ext_ref[rows, cols] if b == 0 else shifted_ref[b - 1, rows, cols]
                acc = acc + cw_ref[j:j + 1, cols] * window
            y_ref[r0:r0 + rc, c0:c0 + LANES] = acc
    _mix_out_tail(y_ref[...], cb_ref, lng_ref, lnb_ref, wco_ref, bco_ref, o_ref, wao_ref, ga_ref, gb_ref, h_ref,
                  g_ref, wout_ref, out_ref)


def _mix_out_sample_kernel(win_ref, cw_ref, cb_ref, lng_ref, lnb_ref, wco_ref, bco_ref, o_ref, wao_ref,
                           ga_ref, gb_ref, h_ref, g_ref, wout_ref, out_ref):
    y = cw_ref[0:1, :] * win_ref[0]
    for j in range(1, CONV_WIDTH):
        y = y + cw_ref[j:j + 1, :] * win_ref[j]
    _mix_out_tail(y, cb_ref, lng_ref, lnb_ref, wco_ref, bco_ref, o_ref, wao_ref, ga_ref, gb_ref, h_ref, g_ref,
                  wout_ref, out_ref)


def _resident(shape):
    return pl.BlockSpec(shape, lambda i: (0,) * len(shape), pipeline_mode=pl.Buffered(1))


def _mix_out_common_specs(tm, c, d, aw, gate):
    row = lambda i: (i, 0)
    g_spec = (pl.BlockSpec((1, d), lambda i: (0, 0)) if gate.shape[0] == 1 else pl.BlockSpec((tm, d), row))
    return [_resident((CONV_HALO, c)), _resident((1, c)), _resident((1, c)), _resident((1, c)),
            _resident((c, d)), _resident((1, d)),
            pl.BlockSpec((tm, aw), row), _resident((aw, d)),
            pl.BlockSpec((tm, d), row), pl.BlockSpec((tm, d), row), pl.BlockSpec((tm, d), row), g_spec,
            _resident((d, d))]


def _mix_out_bytes(tm, c, d, aw, o_dtype):
    return (2 * (_nbytes((tm, aw), o_dtype) + 4 * _nbytes((tm, d), F32) + _nbytes((tm, d), F32))
            + _nbytes((c, d), BF16) + _nbytes((aw, d), BF16) + _nbytes((d, d), BF16)
            + 8 * _nbytes((tm, d), F32))


def _mix_out_prompt(glu, conv_w, conv_b, ln_g, ln_b, w_co, b_co, o, w_ao, ga, gb, h, gate, w_out, *, tm):
    m, c = glu.shape
    d = h.shape[1]
    aw = o.shape[1]
    tm = min(tm, m)
    assert m % tm == 0 and tm % CONV_HALO == 0
    hb = tm // CONV_HALO
    blocks = _mix_out_bytes(tm, c, d, aw, o.dtype) + (3 + SUBLANES) * _nbytes((tm + CONV_HALO, c), F32)
    return pl.pallas_call(
        functools.partial(_mix_out_prompt_kernel, tm=tm),
        out_shape=jax.ShapeDtypeStruct((m, d), F32),
        grid=(m // tm,),
        in_specs=[pl.BlockSpec((tm, c), lambda i: (i, 0)),
                  pl.BlockSpec((CONV_HALO, c), lambda i: (jnp.maximum(i * hb - 1, 0), 0))]
                 + _mix_out_common_specs(tm, c, d, aw, gate),
        out_specs=pl.BlockSpec((tm, d), lambda i: (i, 0)),
        scratch_shapes=[pltpu.VMEM((tm + CONV_HALO, c), F32), pltpu.VMEM((SUBLANES - 1, tm + CONV_HALO, c), F32),
                        pltpu.VMEM((tm, c), F32)],
        compiler_params=pltpu.CompilerParams(dimension_semantics=("arbitrary",),
                                             vmem_limit_bytes=_vmem_limit(blocks)),
        name="mix_out_prompt",
    )(glu, glu, conv_w, conv_b, ln_g, ln_b, w_co, b_co, o, w_ao, ga, gb, h, gate, w_out)


def _mix_out_sample(win, conv_w, conv_b, ln_g, ln_b, w_co, b_co, o, w_ao, ga, gb, h, gate, w_out):
    _, m, c = win.shape
    d = h.shape[1]
    aw = o.shape[1]
    blocks = _mix_out_bytes(m, c, d, aw, o.dtype) + 2 * _nbytes(win.shape, F32)
    return pl.pallas_call(
        _mix_out_sample_kernel,
        out_shape=jax.ShapeDtypeStruct((m, d), F32),
        grid=(1,),
        in_specs=[pl.BlockSpec(win.shape, lambda i: (0, 0, 0))] + _mix_out_common_specs(m, c, d, aw, gate),
        out_specs=pl.BlockSpec((m, d), lambda i: (0, 0)),
        compiler_params=pltpu.CompilerParams(dimension_semantics=("arbitrary",),
                                             vmem_limit_bytes=_vmem_limit(blocks)),
        name="mix_out_sample",
    )(win, conv_w, conv_b, ln_g, ln_b, w_co, b_co, o, w_ao, ga, gb, h, gate, w_out)


def _rope_tables(pos):
    half = ROT_DIM // 2
    inv = jnp.power(ROPE_THETA, -jnp.arange(half, dtype=F32) * 2.0 / ROT_DIM)
    ang = pos.astype(F32)[:, None] * inv[None, :]
    cos, sin = jnp.cos(ang), jnp.sin(ang)
    t = pos.shape[0]
    pad = jnp.zeros((t, HEAD_DIM - ROT_DIM), F32)
    zero = jnp.zeros((t, half), F32)
    cos_t = jnp.concatenate([cos, cos, pad + 1.0], axis=1)
    sin_a = jnp.concatenate([-sin, zero, pad], axis=1)
    sin_b = jnp.concatenate([zero, sin, pad], axis=1)
    rep = LANES // HEAD_DIM
    return jnp.tile(cos_t, (1, rep)), jnp.tile(sin_a, (1, rep)), jnp.tile(sin_b, (1, rep))


def _group_mean_matrix(n):
    g = jnp.arange(n) // HEAD_DIM
    return jnp.where(g[:, None] == g[None, :], 1.0 / HEAD_DIM, 0.0).astype(BF16)


def _half_sum_matrix():
    half = jnp.arange(V_DIM) // HEAD_DIM
    return (half[:, None] == half[None, :]).astype(BF16)


def _half_spread_matrix():
    lane = jnp.arange(LANES)
    col_half = jnp.arange(2 * LANES) // LANES
    return (lane[:, None] == col_half[None, :] * HEAD_DIM).astype(BF16)


def _tile_config(seq, d_ff):
    return dict(ffn_tm=min(512, seq), ffn_tf=min(512, pl.cdiv(d_ff, LANES) * LANES), mix_in_tm=min(512, seq),
                attn_tq=min(4096, seq), attn_tk=min(512, seq), mix_out_tm=min(256, seq))


def kernel(x_prompt, x_sample, cache_k, cache_v, state_conv, page_table, c_prompt, c_sample, w_ada, b_ada, norm1, w1_gate, w1_up, w1_down, norm2, w_in, q_norm, k_norm, lam_q1, lam_k1, lam_q2, lam_k2, subln, w_attn_out, conv_w, conv_b, conv_ln_g, conv_ln_b, w_conv_out, b_conv_out, w_out, norm3, w2_gate, w2_up, w2_down):
    bsz, seq, d = x_prompt.shape
    db, ts, _ = x_sample.shape
    depth = w_ada.shape[0]
    assert bsz == 1 and ts == 1 and depth == 1, "kernel is specialised to one prompt sequence, one new token, one layer"
    c = conv_w.shape[-1]
    d_ff = w1_gate.shape[-1]
    past_len = page_table.shape[1] * PAGE_SIZE
    cfg = _tile_config(seq, d_ff)
    lam_init = _lambda_init(0)

    tf = cfg["ffn_tf"]
    w1g, w1u, w1d = (w[0].astype(BF16) for w in (w1_gate, w1_up, w1_down))
    w2g, w2u, w2d = (w[0].astype(BF16) for w in (w2_gate, w2_up, w2_down))
    w_in_b = w_in[0].astype(BF16)
    w_ao = w_attn_out[0].astype(BF16)
    w_co = w_conv_out[0].astype(BF16)
    w_o = w_out[0].astype(BF16)

    n_c = bsz + db
    c_all = jnp.concatenate([c_prompt, c_sample, jnp.zeros((-n_c % SUBLANES, d), F32)], axis=0)
    ada = _ada(c_all, w_ada[0], b_ada)
    ada_p = [ada[0:1, i * d:(i + 1) * d] for i in range(N_MOD)]
    ada_s = [ada[1:1 + db, i * d:(i + 1) * d] for i in range(N_MOD)]

    qg = jnp.tile(q_norm[0], c // HEAD_DIM)[None]
    kg = jnp.tile(k_norm[0], c // HEAD_DIM)[None]
    grp = _group_mean_matrix(min(MXU_DIM, c))
    red = _half_sum_matrix()
    lam_vecs = jnp.stack([lam_q1[0], lam_k1[0], lam_q2[0], lam_k2[0]])
    cw = jnp.pad(conv_w[0], ((0, CONV_HALO - CONV_WIDTH), (0, 0)))

    def ffn1(x, mods, tm):
        sh1, sc1, g1 = mods[0:3]
        return _ffn(x, norm1, sh1, sc1, g1, w1g, w1u, w1d, tm=tm, tf=tf)

    def ffn2(h, mods, tm):
        sh3, sc3, g3 = mods[6:9]
        return _ffn(h, norm3, sh3, sc3, g3, w2g, w2u, w2d, tm=tm, tf=tf)

    def mix_in(h, mods, pos, tm, transposed):
        cos_t, sin_a, sin_b = _rope_tables(pos)
        return _mix_in(h, norm2, mods[3], mods[4], w_in_b, qg, kg, grp, cos_t, sin_a, sin_b, tm=tm,
                       transposed=transposed)

    xs = x_sample[:, 0]
    pos_s = jnp.full((db,), past_len, jnp.int32)
    hs = ffn1(xs, ada_s, db)
    qs, ks, _, vs, _, glu_s, ga_s, gb_s = mix_in(hs, ada_s, pos_s, db, transposed=False)
    qs3, ks3, vs3 = (a.astype(F32).reshape(db, N_HEADS, V_DIM) for a in (qs, ks, vs))

    state = _decode_pages(page_table, qs3, cache_k.reshape(cache_k.shape[1:]), cache_v.reshape(cache_v.shape[1:]), red,
                          _half_spread_matrix())

    xp = x_prompt[0]
    pos_p = jnp.arange(seq, dtype=jnp.int32)
    hp = ffn1(xp, ada_p, cfg["ffn_tm"])
    qtp, kp, kpb, vp, vtp, glu_p, ga_p, gb_p = mix_in(hp, ada_p, pos_p, cfg["mix_in_tm"], transposed=True)
    op = _prompt_attention(qtp, kpb, vtp, lam_vecs, subln.reshape(V_DIM, 1), tq=cfg["attn_tq"], tk=cfg["attn_tk"],
                           lam_init=lam_init)
    hp2 = _mix_out_prompt(glu_p, cw, conv_b, conv_ln_g, conv_ln_b, w_co, b_conv_out, op, w_ao, ga_p, gb_p, hp,
                          ada_p[5], w_o, tm=cfg["mix_out_tm"])
    yp = ffn2(hp2, ada_p, cfg["ffn_tm"])

    os_ = _decode_finish(qs3, ks3, vs3, state, lam_vecs, subln, red, lam_init=lam_init)
    padded_s = jnp.concatenate([state_conv[0], glu_s[:, None, :]], axis=1)
    win = jnp.pad(padded_s.transpose(1, 0, 2), ((0, CONV_HALO - CONV_WIDTH), (0, 0), (0, 0)))
    hs2 = _mix_out_sample(win, cw, conv_b, conv_ln_g, conv_ln_b, w_co, b_conv_out,
                          os_.reshape(db, ATTN_WIDTH), w_ao, ga_s, gb_s, hs, ada_s[5], w_o)
    ys = ffn2(hs2, ada_s, db)

    return (yp[None],
            ys[:, None, :],
            kp.reshape(1, 1, seq, N_HEADS, 2 * HEAD_DIM),
            vp.reshape(1, 1, seq, N_HEADS, V_DIM),
            glu_p[seq - CONV_STATE:][None, None],
            ks.reshape(1, db, 1, N_HEADS, 2 * HEAD_DIM),
            vs.reshape(1, db, 1, N_HEADS, V_DIM),
            padded_s[None, :, 1:, :])
```
